```python
import jax, jax.numpy as jnp
from jax import lax
import numpy as np

D_MODEL = 1024
BATCH = 8
SEQ = 4096
DEPTH = 2

NORM_EPS = 1e-6
ROPE_THETA = 500000.0

A_HEADS = 4
A_HEAD_DIM = 64
A_WIDTH = A_HEADS * A_HEAD_DIM
A_DECAY_LORA = 64
A_ICLR_LORA = 64
A_GATE_LORA = 160
A_GN_EPS = 64e-5
A_SIZES = (A_WIDTH, A_WIDTH, A_WIDTH, A_DECAY_LORA, A_ICLR_LORA, A_GATE_LORA)
A_PROJ = sum(A_SIZES)

B_HEADS = 4
B_HEAD_DIM = 64
B_WIDTH = B_HEADS * B_HEAD_DIM
B_ROT_DIM = B_HEAD_DIM // 4
B_PATTERNS = ((128, 1), (512, 4), (2048, 16))
B_BLOCK = 128
B_PROJ = 3 * B_WIDTH

C_HEADS = 4
C_NOPE_DIM = 128
C_ROPE_DIM = 64
C_V_DIM = 128
C_Q_LORA = 256
C_KV_LORA = 128
C_WIDTH = C_HEADS * C_V_DIM
C_Q_BLOCK = 128
C_PROJ = C_Q_LORA + C_KV_LORA + C_ROPE_DIM

MIX_WIDTH = A_WIDTH + B_WIDTH + C_WIDTH
P_TOTAL = A_PROJ + B_PROJ + C_PROJ
D_FF = ((8 * D_MODEL + 767) // 768) * 256

kernel_name = "hybrid_rwkv7_dilated_mla_trunk"


def split_sizes(t, sizes):
    idx = [int(i) for i in np.cumsum(sizes)[:-1]]
    return jnp.split(t, idx, axis=-1)


def rms_norm(x, g):
    xf = x.astype(jnp.float32)
    y = xf * lax.rsqrt(jnp.mean(xf * xf, axis=-1, keepdims=True) + NORM_EPS)
    return (y * g.astype(jnp.float32)).astype(x.dtype)


def rope_tables(positions, dim):
    inv_freq = 1.0 / (ROPE_THETA ** (jnp.arange(0, dim, 2, dtype=jnp.float32) / dim))
    ang = positions.astype(jnp.float32)[..., None] * inv_freq
    return jnp.cos(ang), jnp.sin(ang)


def apply_rope(t, cos, sin):
    t1, t2 = jnp.split(t, 2, axis=-1)
    c = cos[:, :, None, :].astype(t.dtype)
    s = sin[:, :, None, :].astype(t.dtype)
    return jnp.concatenate([t1 * c - t2 * s, t1 * s + t2 * c], axis=-1)


def partial_rope(t, cos, sin):
    rot, rest = jnp.split(t, [B_ROT_DIM], axis=-1)
    return jnp.concatenate([apply_rope(rot, cos, sin), rest], axis=-1)


def token_shift(t):
    return jnp.pad(t, ((0, 0), (1, 0), (0, 0)))[:, :-1]


def rwkv7_mixer(pa, mu, w0, decay_up, a0, iclr_up, gate_up, k_k, k_a, r_k, ln_w, ln_b):
    bsz, seq, _ = pa.shape
    pf = pa.astype(jnp.float32)
    pf = pf + (token_shift(pf) - pf) * mu
    r, k, v, xw, xa, xg = split_sizes(pf, A_SIZES)
    w_log = -jax.nn.softplus(-(w0 + jnp.tanh(xw) @ decay_up)) - 0.5
    decay = jnp.exp(-jnp.exp(w_log))
    a = jax.nn.sigmoid(a0 + xa @ iclr_up)
    g = jax.nn.sigmoid(xg) @ gate_up

    def heads(t):
        return t.reshape(bsz, seq, A_HEADS, A_HEAD_DIM)

    kk = heads(k * k_k)
    kk = kk / jnp.maximum(jnp.linalg.norm(kk, axis=-1, keepdims=True), 1e-12)
    k = k * (1.0 + (a - 1.0) * k_a)
    r_h, k_h, v_h, w_h, a_h = heads(r), heads(k), heads(v), heads(decay), heads(a)
    b_h = kk * a_h

    def step(state, inp):
        r_t, w_t, k_t, v_t, kk_t, b_t = inp
        sa = jnp.einsum('bhvk,bhk->bhv', state, kk_t)
        state = (state * w_t[:, :, None, :] - sa[..., None] * b_t[:, :, None, :]
                 + v_t[..., None] * k_t[:, :, None, :])
        y = jnp.einsum('bhvk,bhk->bhv', state, r_t)
        return state, y

    seq_first = lambda t: jnp.moveaxis(t, 1, 0)
    state0 = jnp.zeros((bsz, A_HEADS, A_HEAD_DIM, A_HEAD_DIM), jnp.float32)
    _, y = lax.scan(step, state0, tuple(seq_first(t) for t in (r_h, w_h, k_h, v_h, kk, b_h)))
    y = jnp.moveaxis(y, 0, 1)
    mean = jnp.mean(y, axis=-1, keepdims=True)
    var = jnp.mean(jnp.square(y - mean), axis=-1, keepdims=True)
    y = ((y - mean) * lax.rsqrt(var + A_GN_EPS)).reshape(bsz, seq, A_WIDTH) * ln_w + ln_b
    bonus = jnp.sum(r_h * k_h * r_k, axis=-1, keepdims=True) * v_h
    y = y + bonus.reshape(bsz, seq, A_WIDTH)
    return (y * g).astype(pa.dtype)


def dilated_branch(q, k, v, dil, band):
    bsz, seq, nh, hd = q.shape
    span = dil * B_BLOCK
    seq_pad = -(-seq // span) * span
    pad = ((0, 0), (0, seq_pad - seq), (0, 0), (0, 0))
    q, k, v = (jnp.pad(t, pad) for t in (q, k, v))
    sub = seq_pad // dil
    nb = sub // B_BLOCK

    def to_blocks(t):
        return t.reshape(bsz, sub, dil, nh, hd).transpose(0, 2, 1, 3, 4).reshape(bsz, dil, nb, B_BLOCK, nh, hd)

    def with_prev(t):
        prev = jnp.pad(t, ((0, 0), (0, 0), (1, 0), (0, 0), (0, 0), (0, 0)))[:, :, :-1]
        return jnp.concatenate([prev, t], axis=3)

    qb = to_blocks(q)
    kw, vw = with_prev(to_blocks(k)), with_prev(to_blocks(v))
    s = jnp.einsum('brnqhd,brnkhd->brnhqk', qb, kw).astype(jnp.float32) * (hd ** -0.5)
    qi = jnp.arange(B_BLOCK)[:, None] + B_BLOCK
    ki = jnp.arange(2 * B_BLOCK)[None, :]
    dist = qi - ki
    in_band = (dist >= 0) & (dist <= band)
    has_prev = (jnp.arange(nb) > 0)[:, None, None] | (ki >= B_BLOCK)[None]
    mask = in_band[None] & has_prev
    s = jnp.where(mask[None, None, :, None], s, -jnp.inf)
    lse = jax.nn.logsumexp(s, axis=-1)
    p = jnp.exp(s - lse[..., None])
    o = jnp.einsum('brnhqk,brnkhd->brnqhd', p.astype(v.dtype), vw)
    o = o.reshape(bsz, dil, sub, nh, hd).transpose(0, 2, 1, 3, 4).reshape(bsz, seq_pad, nh, hd)[:, :seq]
    lse = lse.transpose(0, 1, 2, 4, 3).reshape(bsz, dil, sub, nh).transpose(0, 2, 1, 3).reshape(bsz, seq_pad, nh)[:, :seq]
    return o, lse


def dilated_mixer(pb, cos, sin):
    bsz, seq, _ = pb.shape
    q, k, v = (t.reshape(bsz, seq, B_HEADS, B_HEAD_DIM) for t in jnp.split(pb, 3, axis=-1))
    q = partial_rope(q, cos, sin)
    k = partial_rope(k, cos, sin)
    outs, lses = zip(*[dilated_branch(q, k, v, d, w // d) for (w, d) in B_PATTERNS])
    wts = jax.nn.softmax(jnp.stack(lses), axis=0)
    o = jnp.sum(wts[..., None] * jnp.stack(outs).astype(jnp.float32), axis=0)
    return o.reshape(bsz, seq, B_WIDTH).astype(pb.dtype)


def mla_mixer(pc, cos, sin, q_norm_g, kv_norm_g, w_uq, w_ukv):
    bsz, seq, _ = pc.shape
    cq, ckv, k_rope = split_sizes(pc, (C_Q_LORA, C_KV_LORA, C_ROPE_DIM))
    q = (rms_norm(cq, q_norm_g) @ w_uq).reshape(bsz, seq, C_HEADS, C_NOPE_DIM + C_ROPE_DIM)
    kv = (rms_norm(ckv, kv_norm_g) @ w_ukv).reshape(bsz, seq, C_HEADS, C_NOPE_DIM + C_V_DIM)
    q_nope, q_rope = jnp.split(q, [C_NOPE_DIM], axis=-1)
    k_nope, v = jnp.split(kv, [C_NOPE_DIM], axis=-1)
    q_rope = apply_rope(q_rope, cos, sin)
    k_rope = apply_rope(k_rope[:, :, None, :], cos, sin)[:, :, 0]
    nq = seq // C_Q_BLOCK
    scale = (C_NOPE_DIM + C_ROPE_DIM) ** -0.5
    key_pos = jnp.arange(seq)

    def blocks(t):
        return jnp.moveaxis(t.reshape(bsz, nq, C_Q_BLOCK, *t.shape[2:]), 1, 0)

    def attend(args):
        qn, qr, start = args
        s = (jnp.einsum('bqhd,bkhd->bhqk', qn, k_nope)
             + jnp.einsum('bqhd,bkd->bhqk', qr, k_rope)).astype(jnp.float32) * scale
        q_pos = start + jnp.arange(C_Q_BLOCK)
        s = jnp.where(key_pos[None, :] <= q_pos[:, None], s, -jnp.inf)
        p = jax.nn.softmax(s, axis=-1)
        return jnp.einsum('bhqk,bkhd->bqhd', p.astype(v.dtype), v)

    o = lax.map(attend, (blocks(q_nope), blocks(q_rope), jnp.arange(nq) * C_Q_BLOCK))
    return jnp.moveaxis(o, 0, 1).reshape(bsz, seq, C_WIDTH)


def setup_inputs(seed: int = 0) -> dict:
    key = jax.random.key(seed)
    ks = iter(jax.random.split(key, 32))
    L = DEPTH

    def nrm(shape, scale):
        return jax.random.normal(next(ks), shape, jnp.float32) * scale

    def uni(shape, lo, hi):
        return jax.random.uniform(next(ks), shape, jnp.float32, lo, hi)

    return {
        'x': nrm((BATCH, SEQ, D_MODEL), 1.0),
        'positions': jnp.tile(jnp.arange(SEQ, dtype=jnp.int32)[None, :], (BATCH, 1)),
        'attn_norm_g': 1.0 + nrm((L, D_MODEL), 0.02),
        'w_in': nrm((L, D_MODEL, P_TOTAL), D_MODEL ** -0.5),
        'a_mu': uni((L, A_PROJ), 0.0, 1.0),
        'a_w0': uni((L, A_WIDTH), -6.0, -1.0),
        'a_decay_up': nrm((L, A_DECAY_LORA, A_WIDTH), 0.5 * A_DECAY_LORA ** -0.5),
        'a_a0': nrm((L, A_WIDTH), 0.1),
        'a_iclr_up': nrm((L, A_ICLR_LORA, A_WIDTH), A_ICLR_LORA ** -0.5),
        'a_gate_up': nrm((L, A_GATE_LORA, A_WIDTH), A_GATE_LORA ** -0.5),
        'a_k_k': 0.85 + nrm((L, A_WIDTH), 0.02),
        'a_k_a': 1.0 + nrm((L, A_WIDTH), 0.02),
        'a_r_k': nrm((L, A_HEADS, A_HEAD_DIM), 0.1),
        'a_ln_w': 1.0 + nrm((L, A_WIDTH), 0.02),
        'a_ln_b': nrm((L, A_WIDTH), 0.02),
        'c_q_norm_g': 1.0 + nrm((L, C_Q_LORA), 0.02),
        'c_kv_norm_g': 1.0 + nrm((L, C_KV_LORA), 0.02),
        'c_w_uq': nrm((L, C_Q_LORA, C_HEADS * (C_NOPE_DIM + C_ROPE_DIM)), C_Q_LORA ** -0.5),
        'c_w_ukv': nrm((L, C_KV_LORA, C_HEADS * (C_NOPE_DIM + C_V_DIM)), C_KV_LORA ** -0.5),
        'w_out': nrm((L, MIX_WIDTH, D_MODEL), MIX_WIDTH ** -0.5),
        'ffn_norm_g': 1.0 + nrm((L, D_MODEL), 0.02),
        'ffn_w_gate': nrm((L, D_MODEL, D_FF), D_MODEL ** -0.5),
        'ffn_w_up': nrm((L, D_MODEL, D_FF), D_MODEL ** -0.5),
        'ffn_w_down': nrm((L, D_FF, D_MODEL), D_FF ** -0.5),
        'final_norm_g': 1.0 + nrm((D_MODEL,), 0.02),
    }


def reference(x, positions, attn_norm_g, w_in, a_mu, a_w0, a_decay_up, a_a0, a_iclr_up,
              a_gate_up, a_k_k, a_k_a, a_r_k, a_ln_w, a_ln_b, c_q_norm_g, c_kv_norm_g,
              c_w_uq, c_w_ukv, w_out, ffn_norm_g, ffn_w_gate, ffn_w_up, ffn_w_down,
              final_norm_g):
    cos_b, sin_b = rope_tables(positions, B_ROT_DIM)
    cos_c, sin_c = rope_tables(positions, C_ROPE_DIM)
    for l in range(DEPTH):
        h = rms_norm(x, attn_norm_g[l])
        pa, pb, pc = split_sizes(h @ w_in[l], (A_PROJ, B_PROJ, C_PROJ))
        ya = rwkv7_mixer(pa, a_mu[l], a_w0[l], a_decay_up[l], a_a0[l], a_iclr_up[l],
                         a_gate_up[l], a_k_k[l], a_k_a[l], a_r_k[l], a_ln_w[l], a_ln_b[l])
        yb = dilated_mixer(pb, cos_b, sin_b)
        yc = mla_mixer(pc, cos_c, sin_c, c_q_norm_g[l], c_kv_norm_g[l], c_w_uq[l], c_w_ukv[l])
        x = x + jnp.concatenate([ya, yb, yc], axis=-1) @ w_out[l]
        h = rms_norm(x, ffn_norm_g[l])
        x = x + (jax.nn.silu(h @ ffn_w_gate[l]) * (h @ ffn_w_up[l])) @ ffn_w_down[l]
    return rms_norm(x, final_norm_g)
```

```python
import functools

import jax
import jax.numpy as jnp
import numpy as np
from jax import lax
from jax.experimental import pallas as pl
from jax.experimental.pallas import tpu as pltpu

F32 = jnp.float32
BF16 = jnp.bfloat16

D_MODEL = 1024
NORM_EPS = 1e-6
ROPE_THETA = 500000.0

A_HEADS = 4
A_HEAD_DIM = 64
A_WIDTH = 256
A_DECAY_LORA = 64
A_ICLR_LORA = 64
A_GATE_LORA = 160
A_GN_EPS = 64e-5
A_PROJ = 3 * A_WIDTH + A_DECAY_LORA + A_ICLR_LORA + A_GATE_LORA
A_PAD = 1152
A_CHUNK = 64

B_HEADS = 4
B_HEAD_DIM = 64
B_WIDTH = 256
B_ROT_DIM = 16
B_PATTERNS = ((128, 1), (512, 4), (2048, 16))
B_BLOCK = 128

C_HEADS = 4
C_NOPE_DIM = 128
C_ROPE_DIM = 64
C_V_DIM = 128
C_Q_LORA = 256
C_KV_LORA = 128
C_WIDTH = 512
C_QK_PAD = 256
C_PAD = 512

MIX_WIDTH = 1024
P_PAD = A_PAD + 3 * B_WIDTH + C_PAD
D_FF = 2816

VMEM_LIMIT = 56 * 1024 * 1024
NEG_BIG = -1e30


def _cparams(sem):
    return pltpu.CompilerParams(dimension_semantics=sem, vmem_limit_bytes=VMEM_LIMIT)


def _rms(x, g):
    return x * lax.rsqrt(jnp.mean(x * x, axis=-1, keepdims=True) + NORM_EPS) * g


def _dot(a, b):
    return jnp.dot(a.astype(BF16), b.astype(BF16), preferred_element_type=F32)


def _dot_nt(a, b):
    return lax.dot_general(a.astype(BF16), b.astype(BF16), (((1,), (1,)), ((), ())),
                           preferred_element_type=F32)


def _dot_tn(a, b):
    return lax.dot_general(a.astype(BF16), b.astype(BF16), (((0,), (0,)), ((), ())),
                           preferred_element_type=F32)


def _split3(x):
    hi = x.astype(BF16)
    r1 = x - hi.astype(F32)
    mid = r1.astype(BF16)
    lo = (r1 - mid.astype(F32)).astype(BF16)
    return hi, mid, lo


def _dot_x3(a, b_exact):
    hi, mid, lo = _split3(a)
    b = b_exact.astype(BF16)
    return (jnp.dot(hi, b, preferred_element_type=F32)
            + jnp.dot(mid, b, preferred_element_type=F32)
            + jnp.dot(lo, b, preferred_element_type=F32))


def _dot_hi(a, b):
    ah = a.astype(BF16)
    al = (a - ah.astype(F32)).astype(BF16)
    bh = b.astype(BF16)
    bl = (b - bh.astype(F32)).astype(BF16)
    return (jnp.dot(ah, bh, preferred_element_type=F32)
            + jnp.dot(al, bh, preferred_element_type=F32)
            + jnp.dot(ah, bl, preferred_element_type=F32))


def _sigmoid(x):
    return 1.0 / (1.0 + jnp.exp(-x))


def _in_proj_body(x_ref, g_ref, w_ref, cb_ref, sb_ref, pa_ref, qb_ref, kb_ref, vb_ref, pc_ref):
    h = _rms(x_ref[...], g_ref[...])
    y = jnp.dot(h.astype(BF16), w_ref[...], preferred_element_type=F32)
    pa_ref[...] = y[:, :A_PAD]
    cb = cb_ref[...]
    sb = sb_ref[...]
    lane = lax.broadcasted_iota(jnp.int32, (1, B_WIDTH), 1)
    first = (lane % B_HEAD_DIM) < (B_ROT_DIM // 2)

    def rope(t):
        partner = jnp.where(first, pltpu.roll(t, B_WIDTH - B_ROT_DIM // 2, 1),
                            pltpu.roll(t, B_ROT_DIM // 2, 1))
        return t * cb + partner * sb

    o = A_PAD
    qb_ref[...] = (rope(y[:, o:o + B_WIDTH]) * (B_HEAD_DIM ** -0.5)).astype(BF16)
    kb_ref[...] = rope(y[:, o + B_WIDTH:o + 2 * B_WIDTH]).astype(BF16)
    vb_ref[...] = y[:, o + 2 * B_WIDTH:o + 3 * B_WIDTH].astype(BF16)
    pc_ref[...] = y[:, o + 3 * B_WIDTH:]


def _in_proj(x2, g, w, cb, sb, tm):
    n = x2.shape[0]
    row = lambda i: (i, 0)
    const = lambda i: (0, 0)
    return pl.pallas_call(
        _in_proj_body,
        grid=(n // tm,),
        in_specs=[pl.BlockSpec((tm, D_MODEL), row),
                  pl.BlockSpec((1, D_MODEL), const),
                  pl.BlockSpec((D_MODEL, P_PAD), const),
                  pl.BlockSpec((tm, B_WIDTH), row),
                  pl.BlockSpec((tm, B_WIDTH), row)],
        out_specs=[pl.BlockSpec((tm, A_PAD), row),
                   pl.BlockSpec((tm, B_WIDTH), row),
                   pl.BlockSpec((tm, B_WIDTH), row),
                   pl.BlockSpec((tm, B_WIDTH), row),
                   pl.BlockSpec((tm, C_PAD), row)],
        out_shape=[jax.ShapeDtypeStruct((n, A_PAD), F32),
                   jax.ShapeDtypeStruct((n, B_WIDTH), BF16),
                   jax.ShapeDtypeStruct((n, B_WIDTH), BF16),
                   jax.ShapeDtypeStruct((n, B_WIDTH), BF16),
                   jax.ShapeDtypeStruct((n, C_PAD), F32)],
        compiler_params=_cparams(("parallel",)),
        name="in_proj",
    )(x2, g, w, cb, sb)


def _rwkv_body(pa_ref, mu_ref, w0_ref, dup_ref, a0_ref, iup_ref, gup_ref, kk_ref, ka_ref,
               rk_ref, lnw_ref, lnb_ref, out_ref,
               prev_s, h_s, r_s, k_s, v_s, kn_s, b_s, lw_s, g_s, bon_s, *, ts):
    T = A_CHUNK
    HT = A_HEADS * T
    W = A_WIDTH

    @pl.when(pl.program_id(1) == 0)
    def _():
        prev_s[...] = jnp.zeros_like(prev_s)
        h_s[...] = jnp.zeros_like(h_s)

    lane_w = lax.broadcasted_iota(jnp.int32, (W, W), 1)
    row_w = lax.broadcasted_iota(jnp.int32, (W, W), 0)
    ebd = jnp.where(lane_w // A_HEAD_DIM == row_w // A_HEAD_DIM, 1.0, 0.0).astype(F32)

    pa = pa_ref[...]
    rid = lax.broadcasted_iota(jnp.int32, (ts, 1), 0)
    shifted = jnp.where(rid == 0, prev_s[0:1, :], pltpu.roll(pa, 1, 0))
    prev_s[0:1, :] = pa[ts - 1:ts, :]
    pf = pa + (shifted - pa) * mu_ref[...]
    r = pf[:, 0:W]
    k = pf[:, W:2 * W]
    v = pf[:, 2 * W:3 * W]
    xwa = pf[:, 3 * W:3 * W + 128]
    xg = pf[:, 3 * W + 128:]
    dl = _dot_hi(jnp.tanh(xwa), dup_ref[...])
    z = -(w0_ref[...] + dl)
    softplus = jnp.maximum(z, 0.0) + jnp.log(1.0 + jnp.exp(-jnp.abs(z)))
    w_log = -softplus - 0.5
    lw = -jnp.exp(w_log)
    a = _sigmoid(a0_ref[...] + _dot_hi(xwa, iup_ref[...]))
    g = _dot(_sigmoid(xg), gup_ref[...])
    kkv = k * kk_ref[...]
    ss = _dot_x3(kkv * kkv, ebd)
    kn = kkv / jnp.maximum(jnp.sqrt(ss), 1e-12)
    k2 = k * (1.0 + (a - 1.0) * ka_ref[...])
    bon = _dot_x3(r * k2 * rk_ref[...], ebd) * v
    r_s[...] = r
    k_s[...] = k2
    v_s[...] = v
    kn_s[...] = kn
    b_s[...] = kn * a
    lw_s[...] = lw
    g_s[...] = g
    bon_s[...] = bon

    rr = lax.broadcasted_iota(jnp.int32, (HT, HT), 0)
    cc = lax.broadcasted_iota(jnp.int32, (HT, HT), 1)
    strict = rr > cc
    lower = rr >= cc
    eye = rr == cc
    hmask = (rr // T) == (cc // A_HEAD_DIM)
    tr = lax.broadcasted_iota(jnp.int32, (T, T), 0)
    tc = lax.broadcasted_iota(jnp.int32, (T, T), 1)
    ltri = jnp.where(tr >= tc, 1.0, 0.0).astype(F32)
    lnw = lnw_ref[...]
    lnb = lnb_ref[...]

    def stack(x):
        return jnp.where(hmask, jnp.concatenate([x] * A_HEADS, axis=0), 0.0).astype(BF16)

    def chunk(c, carry):
        r0 = pl.multiple_of(c * T, T)
        rows = pl.ds(r0, T)
        rc, kc, vc, knc, bc, lwc = (s[rows, :] for s in (r_s, k_s, v_s, kn_s, b_s, lw_s))
        cum = _cumsum_rows(lwc, ltri)
        tot = cum[T - 1:T, :]
        e_in = jnp.exp(cum)
        e_prev = jnp.exp(cum - lwc)
        e_out = jnp.exp(-cum)
        e_end = jnp.exp(tot - cum)
        rs = stack(rc * e_in)
        ks_ = stack(knc * e_prev)
        bh = stack(bc * e_out)
        kh = stack(kc * e_out)
        bg = stack(bc * e_end)
        kg = stack(kc * e_end)
        vs = stack(vc)

        a4 = _dot_nt(jnp.concatenate([ks_, rs], axis=0), jnp.concatenate([bh, kh], axis=0))
        a_ab = jnp.where(strict, a4[:HT, :HT], 0.0)
        a_ak = jnp.where(strict, a4[:HT, HT:], 0.0)
        a_rb = jnp.where(lower, a4[HT:, :HT], 0.0)
        a_rk = jnp.where(lower, a4[HT:, HT:], 0.0)

        m = jnp.where(eye, 1.0, 0.0) - jnp.where((rr // 2 == cc // 2), a_ab, 0.0)
        s = 2
        while s < T:
            blk = (rr // (2 * s) == cc // (2 * s)) & ((rr // s) % 2 == 1) & ((cc // s) % 2 == 0)
            cs = jnp.where(blk, a_ab, 0.0)
            m = m - _dot(m, _dot(cs, m))
            s *= 2

        x1 = _dot(jnp.concatenate([a_ak, a_rk], axis=0), vs)
        ws = x1[:HT]
        arkv = x1[HT:]
        pq = _dot(m, jnp.concatenate([ks_.astype(F32), ws], axis=1))
        x2 = _dot(a_rb, pq)
        rp = rs.astype(F32) - x2[:, :W]
        y0 = arkv - x2[:, W:]
        gd = _dot_tn(bg, pq)
        kv = _dot_tn(kg, vs)
        gmat = jnp.where(eye, e_in[T - 1:T, :], 0.0) - gd[:, :W]
        dmat = kv - gd[:, W:]
        h0 = h_s[...]
        ys = _dot(rp, h0) + y0
        h_s[...] = _dot(gmat, h0) + dmat
        y = ys[0:T] + ys[T:2 * T] + ys[2 * T:3 * T] + ys[3 * T:4 * T]

        inv_n = 1.0 / A_HEAD_DIM
        mean = _dot_x3(y, ebd) * inv_n
        yc = y - mean
        var = _dot_x3(yc * yc, ebd) * inv_n
        yn = yc * lax.rsqrt(var + A_GN_EPS) * lnw + lnb
        out_ref[rows, :] = (yn + bon_s[rows, :]) * g_s[rows, :]
        return carry

    lax.fori_loop(0, ts // T, chunk, 0)


def _cumsum_rows(x, ltri):
    hi, mid, lo = _split3(x)
    lt = ltri.astype(BF16)
    return (jnp.dot(lt, hi, preferred_element_type=F32)
            + jnp.dot(lt, mid, preferred_element_type=F32)
            + jnp.dot(lt, lo, preferred_element_type=F32))


def _rwkv(pa, p, bsz, seq, ts):
    nt = seq // ts
    row = lambda b, i: (b * nt + i, 0)
    const = lambda b, i: (0, 0)
    vec = lambda w: pl.BlockSpec((1, w), const)
    sq = lambda: pltpu.VMEM((ts, A_WIDTH), F32)
    return pl.pallas_call(
        functools.partial(_rwkv_body, ts=ts),
        grid=(bsz, nt),
        in_specs=[pl.BlockSpec((ts, A_PAD), row), vec(A_PAD), vec(A_WIDTH),
                  pl.BlockSpec((128, A_WIDTH), const), vec(A_WIDTH),
                  pl.BlockSpec((128, A_WIDTH), const),
                  pl.BlockSpec((A_WIDTH, A_WIDTH), const),
                  vec(A_WIDTH), vec(A_WIDTH), vec(A_WIDTH), vec(A_WIDTH), vec(A_WIDTH)],
        out_specs=pl.BlockSpec((ts, A_WIDTH), row),
        out_shape=jax.ShapeDtypeStruct((bsz * seq, A_WIDTH), F32),
        scratch_shapes=[pltpu.VMEM((8, A_PAD), F32), pltpu.VMEM((A_WIDTH, A_WIDTH), F32),
                        sq(), sq(), sq(), sq(), sq(), sq(), sq(), sq()],
        compiler_params=_cparams(("parallel", "arbitrary")),
        name="rwkv7",
    )(pa, p["mu"], p["w0"], p["dup"], p["a0"], p["iup"], p["gup"], p["k_k"], p["k_a"],
      p["r_k"], p["ln_w"], p["ln_b"])


def _dilated_body(q_ref, k_ref, v_ref, o_ref, l_ref, *, nb):
    Q = B_BLOCK
    HQ = B_HEADS * Q
    W = B_WIDTH
    rr = lax.broadcasted_iota(jnp.int32, (HQ, W), 0)
    cc = lax.broadcasted_iota(jnp.int32, (HQ, W), 1)
    hsel = jnp.where((rr // Q) == (cc // B_HEAD_DIM), 1.0, 0.0).astype(BF16)
    qi = rr % Q
    dist = qi + Q - cc
    band2 = (dist >= 0) & (dist <= Q)
    band1 = (lax.broadcasted_iota(jnp.int32, (HQ, Q), 1)
             <= lax.broadcasted_iota(jnp.int32, (HQ, Q), 0) % Q)
    lane_h = lax.broadcasted_iota(jnp.int32, (Q, W), 1) // B_HEAD_DIM

    def attend(q, kw, vw, band):
        qs = jnp.concatenate([q] * B_HEADS, axis=0) * hsel
        s = lax.dot_general(qs, kw, (((1,), (1,)), ((), ())), preferred_element_type=F32)
        s = jnp.where(band, s, NEG_BIG)
        m = jnp.max(s, axis=-1, keepdims=True)
        p = jnp.exp(s - m)
        l = jnp.sum(p, axis=-1, keepdims=True)
        o = jnp.dot(p.astype(BF16), vw, preferred_element_type=F32) / l
        lse = jnp.broadcast_to(m + jnp.log(l), (HQ, W))
        out = o[0:Q]
        lout = lse[0:Q]
        for h in range(1, B_HEADS):
            out = jnp.where(lane_h == h, o[h * Q:(h + 1) * Q], out)
            lout = jnp.where(lane_h == h, lse[h * Q:(h + 1) * Q], lout)
        return out, lout

    out, lout = attend(q_ref[0, 0:Q, :], k_ref[0, 0:Q, :], v_ref[0, 0:Q, :], band1)
    o_ref[0, 0:Q, :] = out
    l_ref[0, 0:Q, :] = lout

    def body(n, carry):
        r0 = pl.multiple_of(n * Q, Q)
        p0 = pl.multiple_of((n - 1) * Q, Q)
        out, lout = attend(q_ref[0, pl.ds(r0, Q), :], k_ref[0, pl.ds(p0, 2 * Q), :],
                           v_ref[0, pl.ds(p0, 2 * Q), :], band2)
        o_ref[0, pl.ds(r0, Q), :] = out
        l_ref[0, pl.ds(r0, Q), :] = lout
        return carry

    lax.fori_loop(1, nb, body, 0)


def _dilated_branch(q, k, v, bsz, seq, dil):
    sub = seq // dil
    view = lambda t: t.reshape(bsz, sub, dil * B_WIDTH)
    spec = pl.BlockSpec((1, sub, B_WIDTH), lambda b, r: (b, 0, r))
    o, l = pl.pallas_call(
        functools.partial(_dilated_body, nb=sub // B_BLOCK),
        grid=(bsz, dil),
        in_specs=[spec, spec, spec],
        out_specs=[spec, spec],
        out_shape=[jax.ShapeDtypeStruct((bsz, sub, dil * B_WIDTH), F32)] * 2,
        compiler_params=_cparams(("parallel", "parallel")),
        name="dilated_d%d" % dil,
    )(view(q), view(k), view(v))
    return o.reshape(bsz * seq, B_WIDTH), l.reshape(bsz * seq, B_WIDTH)


def _mla_up_body(pc_ref, qg_ref, kvg_ref, wq_ref, wkv_ref, cc_ref, sc_ref, q_ref, k_ref, v_ref):
    pc = pc_ref[...]
    cq = _rms(pc[:, :C_Q_LORA], qg_ref[...])
    ckv = _rms(pc[:, C_Q_LORA:C_Q_LORA + C_KV_LORA], kvg_ref[...])
    kr = pc[:, C_Q_LORA + C_KV_LORA:]
    scale = (C_NOPE_DIM + C_ROPE_DIM) ** -0.5
    q = jnp.dot(cq.astype(BF16), wq_ref[...], preferred_element_type=F32) * scale
    kv = jnp.dot(ckv.astype(BF16), wkv_ref[...], preferred_element_type=F32)
    cc = cc_ref[...]
    sc = sc_ref[...]

    def rope(t):
        return t * cc + pltpu.roll(t, 64, 1) * sc

    krr = rope(kr).astype(BF16)
    for h in range(C_HEADS):
        o = h * C_QK_PAD
        q_ref[0, h, :, 0:C_NOPE_DIM] = q[:, o:o + C_NOPE_DIM].astype(BF16)
        q_ref[0, h, :, C_NOPE_DIM:] = rope(q[:, o + C_NOPE_DIM:o + C_QK_PAD]).astype(BF16)
        k_ref[0, h, :, 0:C_NOPE_DIM] = kv[:, h * C_NOPE_DIM:(h + 1) * C_NOPE_DIM].astype(BF16)
        k_ref[0, h, :, C_NOPE_DIM:] = krr
        vo = C_HEADS * C_NOPE_DIM + h * C_V_DIM
        v_ref[0, h] = kv[:, vo:vo + C_V_DIM].astype(BF16)


def _mla_up(pc, qg, kvg, wq, wkv, cc, sc, bsz, seq, ts):
    nt = seq // ts
    row = lambda b, i: (b * nt + i, 0)
    const = lambda b, i: (0, 0)
    hm = lambda b, i: (b, 0, i, 0)
    return pl.pallas_call(
        _mla_up_body,
        grid=(bsz, nt),
        in_specs=[pl.BlockSpec((ts, C_PAD), row),
                  pl.BlockSpec((1, C_Q_LORA), const),
                  pl.BlockSpec((1, C_KV_LORA), const),
                  pl.BlockSpec((C_Q_LORA, C_HEADS * C_QK_PAD), const),
                  pl.BlockSpec((C_KV_LORA, C_HEADS * (C_NOPE_DIM + C_V_DIM)), const),
                  pl.BlockSpec((ts, 128), row),
                  pl.BlockSpec((ts, 128), row)],
        out_specs=[pl.BlockSpec((1, C_HEADS, ts, C_QK_PAD), hm),
                   pl.BlockSpec((1, C_HEADS, ts, C_QK_PAD), hm),
                   pl.BlockSpec((1, C_HEADS, ts, C_V_DIM), hm)],
        out_shape=[jax.ShapeDtypeStruct((bsz, C_HEADS, seq, C_QK_PAD), BF16),
                   jax.ShapeDtypeStruct((bsz, C_HEADS, seq, C_QK_PAD), BF16),
                   jax.ShapeDtypeStruct((bsz, C_HEADS, seq, C_V_DIM), BF16)],
        compiler_params=_cparams(("parallel", "parallel")),
        name="mla_up",
    )(pc, qg, kvg, wq, wkv, cc, sc)


def _flash_body(q_ref, k_ref, v_ref, o_ref, *, tq):
    qi = pl.program_id(2)
    q = q_ref[0, 0]
    rr = lax.broadcasted_iota(jnp.int32, (tq, tq), 0)
    cc = lax.broadcasted_iota(jnp.int32, (tq, tq), 1)
    causal = cc <= rr

    def step(j, carry, masked):
        m, l, acc = carry
        k0 = pl.multiple_of(j * tq, tq)
        kb = k_ref[0, 0, pl.ds(k0, tq), :]
        vb = v_ref[0, 0, pl.ds(k0, tq), :]
        s = lax.dot_general(q, kb, (((1,), (1,)), ((), ())), preferred_element_type=F32)
        if masked:
            s = jnp.where(causal, s, NEG_BIG)
        m_new = jnp.maximum(m, jnp.max(s, axis=-1, keepdims=True))
        alpha = jnp.exp(m - m_new)
        p = jnp.exp(s - m_new)
        l = alpha * l + jnp.sum(p, axis=-1, keepdims=True)
        acc = alpha * acc + jnp.dot(p.astype(BF16), vb, preferred_element_type=F32)
        return m_new, l, acc

    init = (jnp.full((tq, 1), NEG_BIG, F32), jnp.zeros((tq, 1), F32),
            jnp.zeros((tq, C_V_DIM), F32))
    carry = lax.fori_loop(0, qi, lambda j, c: step(j, c, False), init)
    m, l, acc = step(qi, carry, True)
    o_ref[0] = acc / l


def _flash(q, k, v, bsz, seq, tq):
    nq = seq // tq
    return pl.pallas_call(
        functools.partial(_flash_body, tq=tq),
        grid=(bsz, C_HEADS, nq),
        in_specs=[pl.BlockSpec((1, 1, tq, C_QK_PAD), lambda b, h, i: (b, h, i, 0)),
                  pl.BlockSpec((1, 1, seq, C_QK_PAD), lambda b, h, i: (b, h, 0, 0)),
                  pl.BlockSpec((1, 1, seq, C_V_DIM), lambda b, h, i: (b, h, 0, 0))],
        out_specs=pl.BlockSpec((1, tq, C_V_DIM), lambda b, h, i: (b, i, h)),
        out_shape=jax.ShapeDtypeStruct((bsz, seq, C_WIDTH), F32),
        compiler_params=_cparams(("parallel", "parallel", "arbitrary")),
        name="mla_flash",
    )(q, k, v)


def _ffn_body(x_ref, ya_ref, o1_ref, o2_ref, o3_ref, l1_ref, l2_ref, l3_ref, yc_ref,
              wo_ref, g_ref, wg_ref, wu_ref, wd_ref, fg_ref, out_ref,
              xn_s, h_s, acc_s, *, final):
    j = pl.program_id(1)

    @pl.when(j == 0)
    def _():
        l1, l2, l3 = l1_ref[...], l2_ref[...], l3_ref[...]
        mx = jnp.maximum(jnp.maximum(l1, l2), l3)
        e1, e2, e3 = jnp.exp(l1 - mx), jnp.exp(l2 - mx), jnp.exp(l3 - mx)
        yb = (e1 * o1_ref[...] + e2 * o2_ref[...] + e3 * o3_ref[...]) / (e1 + e2 + e3)
        mix = (jnp.dot(ya_ref[...].astype(BF16), wo_ref[0:A_WIDTH, :], preferred_element_type=F32)
               + jnp.dot(yb.astype(BF16), wo_ref[A_WIDTH:A_WIDTH + B_WIDTH, :],
                         preferred_element_type=F32)
               + jnp.dot(yc_ref[...].astype(BF16), wo_ref[A_WIDTH + B_WIDTH:, :],
                         preferred_element_type=F32))
        xn = x_ref[...] + mix
        xn_s[...] = xn
        h_s[...] = _rms(xn, g_ref[...]).astype(BF16)
        acc_s[...] = jnp.zeros_like(acc_s)

    h = h_s[...]
    gt = jnp.dot(h, wg_ref[...], preferred_element_type=F32)
    up = jnp.dot(h, wu_ref[...], preferred_element_type=F32)
    act = gt * _sigmoid(gt) * up
    acc_s[...] += jnp.dot(act.astype(BF16), wd_ref[...], preferred_element_type=F32)

    @pl.when(j == pl.num_programs(1) - 1)
    def _():
        y = xn_s[...] + acc_s[...]
        if final:
            y = _rms(y, fg_ref[...])
        out_ref[...] = y


def _ffn(x2, ya, ob, lb, yc, wo, g, wg, wu, wd, fg, tm, tf, final):
    n = x2.shape[0]
    row = lambda i, j: (i, 0)
    const = lambda i, j: (0, 0)
    rb = lambda w: pl.BlockSpec((tm, w), row)
    return pl.pallas_call(
        functools.partial(_ffn_body, final=final),
        grid=(n // tm, D_FF // tf),
        in_specs=[rb(D_MODEL), rb(A_WIDTH), rb(B_WIDTH), rb(B_WIDTH), rb(B_WIDTH),
                  rb(B_WIDTH), rb(B_WIDTH), rb(B_WIDTH), rb(C_WIDTH),
                  pl.BlockSpec((MIX_WIDTH, D_MODEL), const),
                  pl.BlockSpec((1, D_MODEL), const),
                  pl.BlockSpec((D_MODEL, tf), lambda i, j: (0, j)),
                  pl.BlockSpec((D_MODEL, tf), lambda i, j: (0, j)),
                  pl.BlockSpec((tf, D_MODEL), lambda i, j: (j, 0)),
                  pl.BlockSpec((1, D_MODEL), const)],
        out_specs=pl.BlockSpec((tm, D_MODEL), row),
        out_shape=jax.ShapeDtypeStruct((n, D_MODEL), F32),
        scratch_shapes=[pltpu.VMEM((tm, D_MODEL), F32), pltpu.VMEM((tm, D_MODEL), BF16),
                        pltpu.VMEM((tm, D_MODEL), F32)],
        compiler_params=_cparams(("parallel", "arbitrary")),
        name="out_proj_ffn",
    )(x2, ya, ob[0], ob[1], ob[2], lb[0], lb[1], lb[2], yc, wo, g, wg, wu, wd, fg)


def _pad_cols(w, width):
    return jnp.pad(w, ((0, 0), (0, width - w.shape[1])))


def _rope_layout_cols(w):
    z = jnp.zeros(w.shape[:-1] + (32,), w.dtype)
    return jnp.concatenate([w[..., :32], z, w[..., 32:], z], axis=-1)


def _layer_params(l, w_in, a_mu, a_w0, a_decay_up, a_a0, a_iclr_up, a_gate_up, a_k_k, a_k_a,
                  a_r_k, a_ln_w, a_ln_b, c_w_uq, c_w_ukv):
    wi = w_in[l]
    oa = A_PROJ
    ob = oa + 3 * B_WIDTH
    w_all = jnp.concatenate([
        _pad_cols(wi[:, :oa], A_PAD),
        wi[:, oa:ob],
        wi[:, ob:ob + C_Q_LORA + C_KV_LORA],
        _rope_layout_cols(wi[:, ob + C_Q_LORA + C_KV_LORA:]),
    ], axis=1).astype(BF16)
    z64 = jnp.zeros((64, A_WIDTH), F32)
    rw = lambda t: t.reshape(1, -1)
    wq = c_w_uq[l].reshape(C_Q_LORA, C_HEADS, C_NOPE_DIM + C_ROPE_DIM)
    wq = jnp.concatenate([wq[..., :C_NOPE_DIM], _rope_layout_cols(wq[..., C_NOPE_DIM:])], axis=-1)
    wkv = c_w_ukv[l].reshape(C_KV_LORA, C_HEADS, C_NOPE_DIM + C_V_DIM)
    wkv = jnp.concatenate([wkv[..., :C_NOPE_DIM].reshape(C_KV_LORA, -1),
                           wkv[..., C_NOPE_DIM:].reshape(C_KV_LORA, -1)], axis=1)
    return dict(
        w_all=w_all,
        mu=_pad_cols(rw(a_mu[l]), A_PAD), w0=rw(a_w0[l]),
        dup=jnp.concatenate([a_decay_up[l], z64], axis=0),
        a0=rw(a_a0[l]),
        iup=jnp.concatenate([z64, a_iclr_up[l]], axis=0),
        gup=jnp.pad(a_gate_up[l], ((0, A_WIDTH - A_GATE_LORA), (0, 0))),
        k_k=rw(a_k_k[l]), k_a=rw(a_k_a[l]), r_k=rw(a_r_k[l]),
        ln_w=rw(a_ln_w[l]), ln_b=rw(a_ln_b[l]),
        wq=wq.reshape(C_Q_LORA, C_HEADS * C_QK_PAD).astype(BF16),
        wkv=wkv.astype(BF16),
    )


def _rope_tables(positions):
    n = positions.size
    pos = positions.reshape(n, 1).astype(F32)

    def tables(dim):
        inv_freq = 1.0 / (ROPE_THETA ** (jnp.arange(0, dim, 2, dtype=F32) / dim))
        ang = pos * inv_freq
        return jnp.cos(ang), jnp.sin(ang)

    cb, sb = tables(B_ROT_DIM)
    one = jnp.ones((n, B_HEAD_DIM - B_ROT_DIM), F32)
    cb = jnp.tile(jnp.concatenate([cb, cb, one], axis=1), (1, B_HEADS))
    sb = jnp.tile(jnp.concatenate([-sb, sb, 0.0 * one], axis=1), (1, B_HEADS))
    cc, sc = tables(C_ROPE_DIM)
    z = jnp.zeros_like(cc)
    cc = jnp.concatenate([cc, z, cc, z], axis=1)
    sc = jnp.concatenate([-sc, z, sc, z], axis=1)
    return cb, sb, cc, sc


def kernel(x, positions, attn_norm_g, w_in, a_mu, a_w0, a_decay_up, a_a0, a_iclr_up, a_gate_up, a_k_k, a_k_a, a_r_k, a_ln_w, a_ln_b, c_q_norm_g, c_kv_norm_g, c_w_uq, c_w_ukv, w_out, ffn_norm_g, ffn_w_gate, ffn_w_up, ffn_w_down, final_norm_g):
    bsz, seq, _ = x.shape
    n = bsz * seq
    depth = w_in.shape[0]
    assert seq % max(w for w, _ in B_PATTERNS) == 0
    cb, sb, cc, sc = _rope_tables(positions)
    x2 = x.reshape(n, D_MODEL)
    tm = min(512, seq)
    for l in range(depth):
        p = _layer_params(l, w_in, a_mu, a_w0, a_decay_up, a_a0, a_iclr_up, a_gate_up, a_k_k,
                          a_k_a, a_r_k, a_ln_w, a_ln_b, c_w_uq, c_w_ukv)
        pa, qb, kb, vb, pc = _in_proj(x2, attn_norm_g[l].reshape(1, -1), p["w_all"], cb, sb, tm)
        ya = _rwkv(pa, p, bsz, seq, min(256, seq))
        ob, lb = zip(*[_dilated_branch(qb, kb, vb, bsz, seq, d) for _, d in B_PATTERNS])
        qc, kc, vc = _mla_up(pc, c_q_norm_g[l].reshape(1, -1), c_kv_norm_g[l].reshape(1, -1),
                             p["wq"], p["wkv"], cc, sc, bsz, seq, tm)
        yc = _flash(qc, kc, vc, bsz, seq, min(256, seq)).reshape(n, C_WIDTH)
        x2 = _ffn(x2, ya, ob, lb, yc, w_out[l].astype(BF16), ffn_norm_g[l].reshape(1, -1),
                  ffn_w_gate[l].astype(BF16), ffn_w_up[l].astype(BF16),
                  ffn_w_down[l].astype(BF16), final_norm_g.reshape(1, -1),
                  tm, 256, l == depth - 1)
    return x2.reshape(bsz, seq, D_MODEL)
```

```python
import functools

import jax
import jax.numpy as jnp
from jax import lax
from jax.experimental import pallas as pl
from jax.experimental.pallas import tpu as pltpu

F32 = jnp.float32
BF16 = jnp.bfloat16

D_MODEL = 1024
NORM_EPS = 1e-6
ROPE_THETA = 500000.0

A_HEADS = 4
A_HEAD_DIM = 64
A_WIDTH = 256
A_DECAY_LORA = 64
A_ICLR_LORA = 64
A_GATE_LORA = 160
A_GN_EPS = 64e-5
A_PROJ = 3 * A_WIDTH + A_DECAY_LORA + A_ICLR_LORA + A_GATE_LORA
A_PAD = 1152
A_CHUNK = 64

B_HEADS = 4
B_HEAD_DIM = 64
B_WIDTH = 256
B_ROT_DIM = 16
B_PATTERNS = ((128, 1), (512, 4), (2048, 16))
B_BLOCK = 128

C_HEADS = 4
C_NOPE_DIM = 128
C_ROPE_DIM = 64
C_V_DIM = 128
C_Q_LORA = 256
C_KV_LORA = 128
C_WIDTH = 512
C_QK_PAD = 256
C_PAD = 512

MIX_WIDTH = 1024
P_PAD = A_PAD + 3 * B_WIDTH + C_PAD
D_FF = 2816
FF_CHUNK = 256

VMEM_LIMIT = 56 * 1024 * 1024
NEG_BIG = -1e30


def _cparams(sem):
    return pltpu.CompilerParams(dimension_semantics=sem, vmem_limit_bytes=VMEM_LIMIT)


def _rms(x, g):
    return x * lax.rsqrt(jnp.mean(x * x, axis=-1, keepdims=True) + NORM_EPS) * g


def _dot(a, b):
    return jnp.dot(a.astype(BF16), b.astype(BF16), preferred_element_type=F32)


def _dot_nt(a, b):
    return lax.dot_general(a.astype(BF16), b.astype(BF16), (((1,), (1,)), ((), ())),
                           preferred_element_type=F32)


def _dot_tn(a, b):
    return lax.dot_general(a.astype(BF16), b.astype(BF16), (((0,), (0,)), ((), ())),
                           preferred_element_type=F32)


def _split3(x):
    hi = x.astype(BF16)
    r1 = x - hi.astype(F32)
    mid = r1.astype(BF16)
    lo = (r1 - mid.astype(F32)).astype(BF16)
    return hi, mid, lo


def _dot_x3(a, b_exact):
    hi, mid, lo = _split3(a)
    b = b_exact.astype(BF16)
    return (jnp.dot(hi, b, preferred_element_type=F32)
            + jnp.dot(mid, b, preferred_element_type=F32)
            + jnp.dot(lo, b, preferred_element_type=F32))


def _dot_hi(a, b):
    ah = a.astype(BF16)
    al = (a - ah.astype(F32)).astype(BF16)
    bh = b.astype(BF16)
    bl = (b - bh.astype(F32)).astype(BF16)
    return (jnp.dot(ah, bh, preferred_element_type=F32)
            + jnp.dot(al, bh, preferred_element_type=F32)
            + jnp.dot(ah, bl, preferred_element_type=F32))


def _sigmoid(x):
    return 1.0 / (1.0 + jnp.exp(-x))


def _in_proj_body(x_ref, g_ref, w_ref, cb_ref, sb_ref, pa_ref, qb_ref, kb_ref, vb_ref, pc_ref):
    h = _rms(x_ref[...], g_ref[...])
    y = jnp.dot(h.astype(BF16), w_ref[...], preferred_element_type=F32)
    pa_ref[...] = y[:, :A_PAD]
    cb = cb_ref[...]
    sb = sb_ref[...]
    lane = lax.broadcasted_iota(jnp.int32, (1, B_WIDTH), 1)
    first = (lane % B_HEAD_DIM) < (B_ROT_DIM // 2)

    def rope(t):
        partner = jnp.where(first, pltpu.roll(t, B_WIDTH - B_ROT_DIM // 2, 1),
                            pltpu.roll(t, B_ROT_DIM // 2, 1))
        return t * cb + partner * sb

    o = A_PAD
    qb_ref[...] = (rope(y[:, o:o + B_WIDTH]) * (B_HEAD_DIM ** -0.5)).astype(BF16)
    kb_ref[...] = rope(y[:, o + B_WIDTH:o + 2 * B_WIDTH]).astype(BF16)
    vb_ref[...] = y[:, o + 2 * B_WIDTH:o + 3 * B_WIDTH].astype(BF16)
    pc_ref[...] = y[:, o + 3 * B_WIDTH:]


def _in_proj(x2, g, w, cb, sb, tm):
    n = x2.shape[0]
    row = lambda i: (i, 0)
    const = lambda i: (0, 0)
    return pl.pallas_call(
        _in_proj_body,
        grid=(n // tm,),
        in_specs=[pl.BlockSpec((tm, D_MODEL), row),
                  pl.BlockSpec((1, D_MODEL), const),
                  pl.BlockSpec((D_MODEL, P_PAD), const),
                  pl.BlockSpec((tm, B_WIDTH), row),
                  pl.BlockSpec((tm, B_WIDTH), row)],
        out_specs=[pl.BlockSpec((tm, A_PAD), row),
                   pl.BlockSpec((tm, B_WIDTH), row),
                   pl.BlockSpec((tm, B_WIDTH), row),
                   pl.BlockSpec((tm, B_WIDTH), row),
                   pl.BlockSpec((tm, C_PAD), row)],
        out_shape=[jax.ShapeDtypeStruct((n, A_PAD), F32),
                   jax.ShapeDtypeStruct((n, B_WIDTH), BF16),
                   jax.ShapeDtypeStruct((n, B_WIDTH), BF16),
                   jax.ShapeDtypeStruct((n, B_WIDTH), BF16),
                   jax.ShapeDtypeStruct((n, C_PAD), F32)],
        compiler_params=_cparams(("parallel",)),
        name="in_proj",
    )(x2, g, w, cb, sb)


def _rwkv_body(pa_ref, mu_ref, w0_ref, dup_ref, a0_ref, iup_ref, gup_ref, kk_ref, ka_ref,
               rk_ref, lnw_ref, lnb_ref, out_ref, prev_s, h_s, *, ts):
    T = A_CHUNK
    HT = A_HEADS * T
    W = A_WIDTH
    C = ts // T

    @pl.when(pl.program_id(1) == 0)
    def _():
        prev_s[...] = jnp.zeros_like(prev_s)
        h_s[...] = jnp.zeros_like(h_s)

    lane_w = lax.broadcasted_iota(jnp.int32, (W, W), 1)
    row_w = lax.broadcasted_iota(jnp.int32, (W, W), 0)
    ebd = jnp.where(lane_w // A_HEAD_DIM == row_w // A_HEAD_DIM, 1.0, 0.0).astype(F32)

    pa = pa_ref[...]
    rid = lax.broadcasted_iota(jnp.int32, (ts, 1), 0)
    shifted = jnp.where(rid == 0, prev_s[0:1, :], pltpu.roll(pa, 1, 0))
    prev_s[0:1, :] = pa[ts - 1:ts, :]
    pf = pa + (shifted - pa) * mu_ref[...]
    r = pf[:, 0:W]
    k = pf[:, W:2 * W]
    v = pf[:, 2 * W:3 * W]
    xwa = pf[:, 3 * W:3 * W + 128]
    xg = pf[:, 3 * W + 128:]
    dl = _dot_hi(jnp.tanh(xwa), dup_ref[...])
    z = -(w0_ref[...] + dl)
    softplus = jnp.maximum(z, 0.0) + jnp.log(1.0 + jnp.exp(-jnp.abs(z)))
    w_log = -softplus - 0.5
    lw = -jnp.exp(w_log)
    a = _sigmoid(a0_ref[...] + _dot_hi(xwa, iup_ref[...]))
    g = _dot(_sigmoid(xg), gup_ref[...])
    kkv = k * kk_ref[...]
    ss = _dot_x3(kkv * kkv, ebd)
    kn = kkv / jnp.maximum(jnp.sqrt(ss), 1e-12)
    k2 = k * (1.0 + (a - 1.0) * ka_ref[...])
    bon = _dot_x3(r * k2 * rk_ref[...], ebd) * v
    b = kn * a

    rr = lax.broadcasted_iota(jnp.int32, (HT, HT), 0)
    cc = lax.broadcasted_iota(jnp.int32, (HT, HT), 1)
    strict = rr > cc
    lower = rr >= cc
    eye = rr == cc
    hmask = (rr // T) == (cc // A_HEAD_DIM)
    tr = lax.broadcasted_iota(jnp.int32, (T, T), 0)
    tc = lax.broadcasted_iota(jnp.int32, (T, T), 1)
    ltri = jnp.where(tr >= tc, 1.0, 0.0).astype(BF16)

    def stack(x):
        return jnp.where(hmask, jnp.concatenate([x] * A_HEADS, axis=0), 0.0).astype(BF16)

    def chunks(x):
        return [x[c * T:(c + 1) * T, :] for c in range(C)]

    def each(fn, *lists):
        return [fn(*xs) for xs in zip(*lists)]

    rc, kc, vc, knc, bc, lwc = (chunks(t) for t in (r, k2, v, kn, b, lw))
    cum = each(lambda x: _cumsum_rows(x, ltri), lwc)
    e_in = each(jnp.exp, cum)
    rs = each(lambda x, e: stack(x * e), rc, e_in)
    ks_ = each(lambda x, cu, l: stack(x * jnp.exp(cu - l)), knc, cum, lwc)
    e_out = each(lambda cu: jnp.exp(-cu), cum)
    bh = each(lambda x, e: stack(x * e), bc, e_out)
    kh = each(lambda x, e: stack(x * e), kc, e_out)
    e_end = each(lambda cu: jnp.exp(cu[T - 1:T, :] - cu), cum)
    bg = each(lambda x, e: stack(x * e), bc, e_end)
    kg = each(lambda x, e: stack(x * e), kc, e_end)
    vs = each(stack, vc)

    a4 = each(lambda k_, r_, b_, kh_: _dot_nt(jnp.concatenate([k_, r_], axis=0),
                                              jnp.concatenate([b_, kh_], axis=0)), ks_, rs, bh, kh)
    a_ab = each(lambda t: jnp.where(strict, t[:HT, :HT], 0.0), a4)
    a_akrk = each(lambda t: jnp.concatenate([jnp.where(strict, t[:HT, HT:], 0.0),
                                             jnp.where(lower, t[HT:, HT:], 0.0)], axis=0), a4)
    a_rb = each(lambda t: jnp.where(lower, t[HT:, :HT], 0.0), a4)

    m = each(lambda t: jnp.where(eye, 1.0, 0.0) - jnp.where((rr // 2 == cc // 2), t, 0.0), a_ab)
    s = 2
    while s < T:
        blk = (rr // (2 * s) == cc // (2 * s)) & ((rr // s) % 2 == 1) & ((cc // s) % 2 == 0)
        cm = each(lambda t, m_: _dot(jnp.where(blk, t, 0.0), m_), a_ab, m)
        m = each(lambda m_, t: m_ - _dot(m_, t), m, cm)
        s *= 2

    x1 = each(_dot, a_akrk, vs)
    pq = each(lambda m_, k_, x: _dot(m_, jnp.concatenate([k_, x[:HT].astype(BF16)], axis=1)),
              m, ks_, x1)
    x2 = each(_dot, a_rb, pq)
    rp = each(lambda r_, x: r_.astype(F32) - x[:, :W], rs, x2)
    y0 = each(lambda x1_, x: x1_[HT:] - x[:, W:], x1, x2)
    gd = each(_dot_tn, bg, pq)
    kv = each(_dot_tn, kg, vs)
    gmat = each(lambda e, t: jnp.where(eye, e[T - 1:T, :], 0.0) - t[:, :W], e_in, gd)
    dmat = each(lambda kv_, t: kv_ - t[:, W:], kv, gd)

    h = h_s[...]
    ys = []
    for c in range(C):
        ys.append(_dot(rp[c], h) + y0[c])
        h = _dot(gmat[c], h) + dmat[c]
    h_s[...] = h
    y = jnp.concatenate([t[0:T] + t[T:2 * T] + t[2 * T:3 * T] + t[3 * T:4 * T] for t in ys], axis=0)

    inv_n = 1.0 / A_HEAD_DIM
    mean = _dot_x3(y, ebd) * inv_n
    yc = y - mean
    var = _dot_x3(yc * yc, ebd) * inv_n
    yn = yc * lax.rsqrt(var + A_GN_EPS) * lnw_ref[...] + lnb_ref[...]
    out_ref[...] = (yn + bon) * g


def _cumsum_rows(x, ltri):
    hi, mid, lo = _split3(x)
    return (jnp.dot(ltri, hi, preferred_element_type=F32)
            + jnp.dot(ltri, mid, preferred_element_type=F32)
            + jnp.dot(ltri, lo, preferred_element_type=F32))


def _rwkv(pa, p, bsz, seq, ts):
    nt = seq // ts
    row = lambda b, i: (b * nt + i, 0)
    const = lambda b, i: (0, 0)
    vec = lambda w: pl.BlockSpec((1, w), const)
    return pl.pallas_call(
        functools.partial(_rwkv_body, ts=ts),
        grid=(bsz, nt),
        in_specs=[pl.BlockSpec((ts, A_PAD), row), vec(A_PAD), vec(A_WIDTH),
                  pl.BlockSpec((128, A_WIDTH), const), vec(A_WIDTH),
                  pl.BlockSpec((128, A_WIDTH), const),
                  pl.BlockSpec((A_WIDTH, A_WIDTH), const),
                  vec(A_WIDTH), vec(A_WIDTH), vec(A_WIDTH), vec(A_WIDTH), vec(A_WIDTH)],
        out_specs=pl.BlockSpec((ts, A_WIDTH), row),
        out_shape=jax.ShapeDtypeStruct((bsz * seq, A_WIDTH), F32),
        scratch_shapes=[pltpu.VMEM((8, A_PAD), F32), pltpu.VMEM((A_WIDTH, A_WIDTH), F32)],
        compiler_params=_cparams(("parallel", "arbitrary")),
        name="rwkv7",
    )(pa, p["mu"], p["w0"], p["dup"], p["a0"], p["iup"], p["gup"], p["k_k"], p["k_a"],
      p["r_k"], p["ln_w"], p["ln_b"])


def _dilated_body(q_ref, k_ref, v_ref, o_ref, l_ref, *, nb):
    Q = B_BLOCK
    HQ = B_HEADS * Q
    W = B_WIDTH
    rr = lax.broadcasted_iota(jnp.int32, (HQ, W), 0)
    cc = lax.broadcasted_iota(jnp.int32, (HQ, W), 1)
    hsel = jnp.where((rr // Q) == (cc // B_HEAD_DIM), 1.0, 0.0).astype(BF16)
    qi = rr % Q
    dist = qi + Q - cc
    band2 = (dist >= 0) & (dist <= Q)
    band1 = (lax.broadcasted_iota(jnp.int32, (HQ, Q), 1)
             <= lax.broadcasted_iota(jnp.int32, (HQ, Q), 0) % Q)
    lane_h = lax.broadcasted_iota(jnp.int32, (Q, W), 1) // B_HEAD_DIM

    def attend(q, kw, vw, band):
        qs = jnp.concatenate([q] * B_HEADS, axis=0) * hsel
        s = lax.dot_general(qs, kw, (((1,), (1,)), ((), ())), preferred_element_type=F32)
        s = jnp.where(band, s, NEG_BIG)
        m = jnp.max(s, axis=-1, keepdims=True)
        p = jnp.exp(s - m)
        l = jnp.sum(p, axis=-1, keepdims=True)
        o = jnp.dot(p.astype(BF16), vw, preferred_element_type=F32) / l
        lse = jnp.broadcast_to(m + jnp.log(l), (HQ, W))
        out = o[0:Q]
        lout = lse[0:Q]
        for h in range(1, B_HEADS):
            out = jnp.where(lane_h == h, o[h * Q:(h + 1) * Q], out)
            lout = jnp.where(lane_h == h, lse[h * Q:(h + 1) * Q], lout)
        return out, lout

    out, lout = attend(q_ref[0, 0:Q, :], k_ref[0, 0:Q, :], v_ref[0, 0:Q, :], band1)
    o_ref[0, 0:Q, :] = out
    l_ref[0, 0:Q, :] = lout

    def body(n, carry):
        r0 = pl.multiple_of(n * Q, Q)
        p0 = pl.multiple_of((n - 1) * Q, Q)
        out, lout = attend(q_ref[0, pl.ds(r0, Q), :], k_ref[0, pl.ds(p0, 2 * Q), :],
                           v_ref[0, pl.ds(p0, 2 * Q), :], band2)
        o_ref[0, pl.ds(r0, Q), :] = out
        l_ref[0, pl.ds(r0, Q), :] = lout
        return carry

    lax.fori_loop(1, nb, body, 0)


def _dilated_branch(q, k, v, bsz, seq, dil):
    sub = seq // dil
    view = lambda t: t.reshape(bsz, sub, dil * B_WIDTH)
    spec = pl.BlockSpec((1, sub, B_WIDTH), lambda b, r: (b, 0, r))
    o, l = pl.pallas_call(
        functools.partial(_dilated_body, nb=sub // B_BLOCK),
        grid=(bsz, dil),
        in_specs=[spec, spec, spec],
        out_specs=[spec, spec],
        out_shape=[jax.ShapeDtypeStruct((bsz, sub, dil * B_WIDTH), F32)] * 2,
        compiler_params=_cparams(("parallel", "parallel")),
        name="dilated_d%d" % dil,
    )(view(q), view(k), view(v))
    return o.reshape(bsz * seq, B_WIDTH), l.reshape(bsz * seq, B_WIDTH)


def _mla_up_body(pc_ref, qg_ref, kvg_ref, wq_ref, wkv_ref, cc_ref, sc_ref, q_ref, k_ref, v_ref):
    pc = pc_ref[...]
    cq = _rms(pc[:, :C_Q_LORA], qg_ref[...])
    ckv = _rms(pc[:, C_Q_LORA:C_Q_LORA + C_KV_LORA], kvg_ref[...])
    kr = pc[:, C_Q_LORA + C_KV_LORA:]
    scale = (C_NOPE_DIM + C_ROPE_DIM) ** -0.5
    q = jnp.dot(cq.astype(BF16), wq_ref[...], preferred_element_type=F32) * scale
    kv = jnp.dot(ckv.astype(BF16), wkv_ref[...], preferred_element_type=F32)
    cc = cc_ref[...]
    sc = sc_ref[...]

    def rope(t):
        return t * cc + pltpu.roll(t, 64, 1) * sc

    krr = rope(kr).astype(BF16)
    for h in range(C_HEADS):
        o = h * C_QK_PAD
        q_ref[0, h, :, 0:C_NOPE_DIM] = q[:, o:o + C_NOPE_DIM].astype(BF16)
        q_ref[0, h, :, C_NOPE_DIM:] = rope(q[:, o + C_NOPE_DIM:o + C_QK_PAD]).astype(BF16)
        k_ref[0, h, :, 0:C_NOPE_DIM] = kv[:, h * C_NOPE_DIM:(h + 1) * C_NOPE_DIM].astype(BF16)
        k_ref[0, h, :, C_NOPE_DIM:] = krr
        vo = C_HEADS * C_NOPE_DIM + h * C_V_DIM
        v_ref[0, h] = kv[:, vo:vo + C_V_DIM].astype(BF16)


def _mla_up(pc, qg, kvg, wq, wkv, cc, sc, bsz, seq, ts):
    nt = seq // ts
    row = lambda b, i: (b * nt + i, 0)
    const = lambda b, i: (0, 0)
    hm = lambda b, i: (b, 0, i, 0)
    return pl.pallas_call(
        _mla_up_body,
        grid=(bsz, nt),
        in_specs=[pl.BlockSpec((ts, C_PAD), row),
                  pl.BlockSpec((1, C_Q_LORA), const),
                  pl.BlockSpec((1, C_KV_LORA), const),
                  pl.BlockSpec((C_Q_LORA, C_HEADS * C_QK_PAD), const),
                  pl.BlockSpec((C_KV_LORA, C_HEADS * (C_NOPE_DIM + C_V_DIM)), const),
                  pl.BlockSpec((ts, 128), row),
                  pl.BlockSpec((ts, 128), row)],
        out_specs=[pl.BlockSpec((1, C_HEADS, ts, C_QK_PAD), hm),
                   pl.BlockSpec((1, C_HEADS, ts, C_QK_PAD), hm),
                   pl.BlockSpec((1, C_HEADS, ts, C_V_DIM), hm)],
        out_shape=[jax.ShapeDtypeStruct((bsz, C_HEADS, seq, C_QK_PAD), BF16),
                   jax.ShapeDtypeStruct((bsz, C_HEADS, seq, C_QK_PAD), BF16),
                   jax.ShapeDtypeStruct((bsz, C_HEADS, seq, C_V_DIM), BF16)],
        compiler_params=_cparams(("parallel", "parallel")),
        name="mla_up",
    )(pc, qg, kvg, wq, wkv, cc, sc)


def _flash_body(q_ref, k_ref, v_ref, o_ref, m_s, l_s, acc_s, sa_s, sb_s, *, tq):
    qi = pl.program_id(2)
    q = q_ref[0, 0]
    rr = lax.broadcasted_iota(jnp.int32, (tq, tq), 0)
    cc = lax.broadcasted_iota(jnp.int32, (tq, tq), 1)
    causal = cc <= rr
    reps = tq // 128
    m_s[...] = jnp.full_like(m_s, NEG_BIG)
    l_s[...] = jnp.zeros_like(l_s)
    acc_s[...] = jnp.zeros_like(acc_s)

    def scores(j, s_ref):
        k0 = pl.multiple_of(j * tq, tq)
        kb = k_ref[0, 0, pl.ds(k0, tq), :]
        s_ref[...] = lax.dot_general(q, kb, (((1,), (1,)), ((), ())), preferred_element_type=F32)

    def update(j, s_ref, masked):
        k0 = pl.multiple_of(j * tq, tq)
        vb = v_ref[0, 0, pl.ds(k0, tq), :]
        s = s_ref[...]
        if masked:
            s = jnp.where(causal, s, NEG_BIG)
        m_prev = m_s[...]
        m_new = jnp.maximum(m_prev, jnp.max(s, axis=-1, keepdims=True))
        alpha = jnp.exp(m_prev - m_new)
        p = jnp.exp(s - jnp.concatenate([m_new] * reps, axis=1))
        l_s[...] = alpha * l_s[...] + jnp.sum(p, axis=-1, keepdims=True)
        acc_s[...] = alpha * acc_s[...] + jnp.dot(p.astype(BF16), vb, preferred_element_type=F32)
        m_s[...] = m_new

    scores(0, sa_s)

    def pair(t, carry):
        scores(2 * t + 1, sb_s)
        update(2 * t, sa_s, False)
        scores(2 * t + 2, sa_s)
        update(2 * t + 1, sb_s, False)
        return carry

    lax.fori_loop(0, qi // 2, pair, 0)

    @pl.when(qi % 2 == 0)
    def _():
        update(qi, sa_s, True)

    @pl.when(qi % 2 == 1)
    def _():
        scores(qi, sb_s)
        update(qi - 1, sa_s, False)
        update(qi, sb_s, True)

    o_ref[0] = acc_s[...] / l_s[...]


def _flash(q, k, v, bsz, seq, tq):
    nq = seq // tq
    stat = lambda: pltpu.VMEM((tq, C_V_DIM), F32)
    return pl.pallas_call(
        functools.partial(_flash_body, tq=tq),
        grid=(bsz, C_HEADS, nq),
        in_specs=[pl.BlockSpec((1, 1, tq, C_QK_PAD), lambda b, h, i: (b, h, i, 0)),
                  pl.BlockSpec((1, 1, seq, C_QK_PAD), lambda b, h, i: (b, h, 0, 0)),
                  pl.BlockSpec((1, 1, seq, C_V_DIM), lambda b, h, i: (b, h, 0, 0))],
        out_specs=pl.BlockSpec((1, tq, C_V_DIM), lambda b, h, i: (b, i, h)),
        out_shape=jax.ShapeDtypeStruct((bsz, seq, C_WIDTH), F32),
        scratch_shapes=[stat(), stat(), stat(), pltpu.VMEM((tq, tq), F32), pltpu.VMEM((tq, tq), F32)],
        compiler_params=_cparams(("parallel", "parallel", "arbitrary")),
        name="mla_flash",
    )(q, k, v)


def _ffn_body(x_ref, ya_ref, o1_ref, o2_ref, o3_ref, l1_ref, l2_ref, l3_ref, yc_ref,
              wo_ref, g_ref, wg_ref, wu_ref, wd_ref, fg_ref, out_ref, h_s, *, final):
    l1, l2, l3 = l1_ref[...], l2_ref[...], l3_ref[...]
    mx = jnp.maximum(jnp.maximum(l1, l2), l3)
    e1, e2, e3 = jnp.exp(l1 - mx), jnp.exp(l2 - mx), jnp.exp(l3 - mx)
    yb = (e1 * o1_ref[...] + e2 * o2_ref[...] + e3 * o3_ref[...]) / (e1 + e2 + e3)
    mix = (jnp.dot(ya_ref[...].astype(BF16), wo_ref[0:A_WIDTH, :], preferred_element_type=F32)
           + jnp.dot(yb.astype(BF16), wo_ref[A_WIDTH:A_WIDTH + B_WIDTH, :],
                     preferred_element_type=F32)
           + jnp.dot(yc_ref[...].astype(BF16), wo_ref[A_WIDTH + B_WIDTH:, :],
                     preferred_element_type=F32))
    xn = x_ref[...] + mix
    out_ref[...] = xn
    h_s[...] = _rms(xn, g_ref[...]).astype(BF16)

    def step(j, carry):
        h = h_s[...]
        gt = jnp.dot(h, wg_ref[j], preferred_element_type=F32)
        up = jnp.dot(h, wu_ref[j], preferred_element_type=F32)
        act = gt * _sigmoid(gt) * up
        out_ref[...] += jnp.dot(act.astype(BF16), wd_ref[j], preferred_element_type=F32)
        return carry

    lax.fori_loop(0, wg_ref.shape[0], step, 0)
    if final:
        out_ref[...] = _rms(out_ref[...], fg_ref[...])


def _ffn(x2, ya, ob, lb, yc, wo, g, wg, wu, wd, fg, tm, final):
    n = x2.shape[0]
    row = lambda i: (i, 0)
    rb = lambda w: pl.BlockSpec((tm, w), row)
    res = lambda a: pl.BlockSpec(a.shape, lambda i: (0,) * a.ndim, pipeline_mode=pl.Buffered(1))
    return pl.pallas_call(
        functools.partial(_ffn_body, final=final),
        grid=(n // tm,),
        in_specs=[rb(D_MODEL), rb(A_WIDTH), rb(B_WIDTH), rb(B_WIDTH), rb(B_WIDTH),
                  rb(B_WIDTH), rb(B_WIDTH), rb(B_WIDTH), rb(C_WIDTH),
                  res(wo), res(g), res(wg), res(wu), res(wd), res(fg)],
        out_specs=pl.BlockSpec((tm, D_MODEL), row),
        out_shape=jax.ShapeDtypeStruct((n, D_MODEL), F32),
        scratch_shapes=[pltpu.VMEM((tm, D_MODEL), BF16)],
        compiler_params=_cparams(("parallel",)),
        name="out_proj_ffn",
    )(x2, ya, ob[0], ob[1], ob[2], lb[0], lb[1], lb[2], yc, wo, g, wg, wu, wd, fg)


def _pad_cols(w, width):
    return jnp.pad(w, ((0, 0), (0, width - w.shape[1])))


def _rope_layout_cols(w):
    z = jnp.zeros(w.shape[:-1] + (32,), w.dtype)
    return jnp.concatenate([w[..., :32], z, w[..., 32:], z], axis=-1)


def _layer_params(l, w_in, a_mu, a_w0, a_decay_up, a_a0, a_iclr_up, a_gate_up, a_k_k, a_k_a,
                  a_r_k, a_ln_w, a_ln_b, c_w_uq, c_w_ukv):
    wi = w_in[l]
    oa = A_PROJ
    ob = oa + 3 * B_WIDTH
    w_all = jnp.concatenate([
        _pad_cols(wi[:, :oa], A_PAD),
        wi[:, oa:ob],
        wi[:, ob:ob + C_Q_LORA + C_KV_LORA],
        _rope_layout_cols(wi[:, ob + C_Q_LORA + C_KV_LORA:]),
    ], axis=1).astype(BF16)
    z64 = jnp.zeros((64, A_WIDTH), F32)
    rw = lambda t: t.reshape(1, -1)
    wq = c_w_uq[l].reshape(C_Q_LORA, C_HEADS, C_NOPE_DIM + C_ROPE_DIM)
    wq = jnp.concatenate([wq[..., :C_NOPE_DIM], _rope_layout_cols(wq[..., C_NOPE_DIM:])], axis=-1)
    wkv = c_w_ukv[l].reshape(C_KV_LORA, C_HEADS, C_NOPE_DIM + C_V_DIM)
    wkv = jnp.concatenate([wkv[..., :C_NOPE_DIM].reshape(C_KV_LORA, -1),
                           wkv[..., C_NOPE_DIM:].reshape(C_KV_LORA, -1)], axis=1)
    return dict(
        w_all=w_all,
        mu=_pad_cols(rw(a_mu[l]), A_PAD), w0=rw(a_w0[l]),
        dup=jnp.concatenate([a_decay_up[l], z64], axis=0),
        a0=rw(a_a0[l]),
        iup=jnp.concatenate([z64, a_iclr_up[l]], axis=0),
        gup=jnp.pad(a_gate_up[l], ((0, A_WIDTH - A_GATE_LORA), (0, 0))),
        k_k=rw(a_k_k[l]), k_a=rw(a_k_a[l]), r_k=rw(a_r_k[l]),
        ln_w=rw(a_ln_w[l]), ln_b=rw(a_ln_b[l]),
        wq=wq.reshape(C_Q_LORA, C_HEADS * C_QK_PAD).astype(BF16),
        wkv=wkv.astype(BF16),
    )


def _rope_tables(positions):
    n = positions.size
    pos = positions.reshape(n, 1).astype(F32)

    def tables(dim):
        inv_freq = 1.0 / (ROPE_THETA ** (jnp.arange(0, dim, 2, dtype=F32) / dim))
        ang = pos * inv_freq
        return jnp.cos(ang), jnp.sin(ang)

    cb, sb = tables(B_ROT_DIM)
    one = jnp.ones((n, B_HEAD_DIM - B_ROT_DIM), F32)
    cb = jnp.tile(jnp.concatenate([cb, cb, one], axis=1), (1, B_HEADS))
    sb = jnp.tile(jnp.concatenate([-sb, sb, 0.0 * one], axis=1), (1, B_HEADS))
    cc, sc = tables(C_ROPE_DIM)
    z = jnp.zeros_like(cc)
    cc = jnp.concatenate([cc, z, cc, z], axis=1)
    sc = jnp.concatenate([-sc, z, sc, z], axis=1)
    return cb, sb, cc, sc


def kernel(x, positions, attn_norm_g, w_in, a_mu, a_w0, a_decay_up, a_a0, a_iclr_up, a_gate_up, a_k_k, a_k_a, a_r_k, a_ln_w, a_ln_b, c_q_norm_g, c_kv_norm_g, c_w_uq, c_w_ukv, w_out, ffn_norm_g, ffn_w_gate, ffn_w_up, ffn_w_down, final_norm_g):
    bsz, seq, _ = x.shape
    n = bsz * seq
    depth = w_in.shape[0]
    assert seq % max(w for w, _ in B_PATTERNS) == 0
    cb, sb, cc, sc = _rope_tables(positions)
    x2 = x.reshape(n, D_MODEL)
    tm = min(512, seq)
    for l in range(depth):
        p = _layer_params(l, w_in, a_mu, a_w0, a_decay_up, a_a0, a_iclr_up, a_gate_up, a_k_k,
                          a_k_a, a_r_k, a_ln_w, a_ln_b, c_w_uq, c_w_ukv)
        pa, qb, kb, vb, pc = _in_proj(x2, attn_norm_g[l].reshape(1, -1), p["w_all"], cb, sb, tm)
        ya = _rwkv(pa, p, bsz, seq, min(256, seq))
        ob, lb = zip(*[_dilated_branch(qb, kb, vb, bsz, seq, d) for _, d in B_PATTERNS])
        qc, kc, vc = _mla_up(pc, c_q_norm_g[l].reshape(1, -1), c_kv_norm_g[l].reshape(1, -1),
                             p["wq"], p["wkv"], cc, sc, bsz, seq, tm)
        yc = _flash(qc, kc, vc, bsz, seq, min(512, seq)).reshape(n, C_WIDTH)
        nf = D_FF // FF_CHUNK
        ffw = lambda w: w.astype(BF16).reshape(D_MODEL, nf, FF_CHUNK).transpose(1, 0, 2)
        x2 = _ffn(x2, ya, ob, lb, yc, w_out[l].astype(BF16), ffn_norm_g[l].reshape(1, -1),
                  ffw(ffn_w_gate[l]), ffw(ffn_w_up[l]),
                  ffn_w_down[l].astype(BF16).reshape(nf, FF_CHUNK, D_MODEL),
                  final_norm_g.reshape(1, -1), tm, l == depth - 1)
    return x2.reshape(bsz, seq, D_MODEL)
```

```python
import functools

import jax
import jax.numpy as jnp
from jax import lax
from jax.experimental import pallas as pl
from jax.experimental.pallas import tpu as pltpu

F32 = jnp.float32
BF16 = jnp.bfloat16

D_MODEL = 1024
NORM_EPS = 1e-6
ROPE_THETA = 500000.0

A_HEADS = 4
A_HEAD_DIM = 64
A_WIDTH = 256
A_DECAY_LORA = 64
A_ICLR_LORA = 64
A_GATE_LORA = 160
A_GN_EPS = 64e-5
A_PROJ = 3 * A_WIDTH + A_DECAY_LORA + A_ICLR_LORA + A_GATE_LORA
A_PAD = 1152
A_CHUNK = 64
RWKV_TILE = 512

B_HEADS = 4
B_HEAD_DIM = 64
B_WIDTH = 256
B_ROT_DIM = 16
B_PATTERNS = ((128, 1), (512, 4), (2048, 16))
B_BLOCK = 128
DIL_UNROLL = 2

C_HEADS = 4
C_NOPE_DIM = 128
C_ROPE_DIM = 64
C_V_DIM = 128
C_Q_LORA = 256
C_KV_LORA = 128
C_WIDTH = 512
C_QK_PAD = 256
C_PAD = 512

MIX_WIDTH = 1024
P_PAD = A_PAD + 3 * B_WIDTH + C_PAD
D_FF = 2816
FF_CHUNK = 256

VMEM_LIMIT = 56 * 1024 * 1024
NEG_BIG = -1e30


def _cparams(sem):
    return pltpu.CompilerParams(dimension_semantics=sem, vmem_limit_bytes=VMEM_LIMIT)


def _rms(x, g):
    return x * lax.rsqrt(jnp.mean(x * x, axis=-1, keepdims=True) + NORM_EPS) * g


def _dot(a, b):
    return jnp.dot(a.astype(BF16), b.astype(BF16), preferred_element_type=F32)


def _dot_nt(a, b):
    return lax.dot_general(a.astype(BF16), b.astype(BF16), (((1,), (1,)), ((), ())),
                           preferred_element_type=F32)


def _dot_tn(a, b):
    return lax.dot_general(a.astype(BF16), b.astype(BF16), (((0,), (0,)), ((), ())),
                           preferred_element_type=F32)


def _split3(x):
    hi = x.astype(BF16)
    r1 = x - hi.astype(F32)
    mid = r1.astype(BF16)
    lo = (r1 - mid.astype(F32)).astype(BF16)
    return hi, mid, lo


def _dot_x2(a, b_exact):
    hi, mid, _ = _split3(a)
    b = b_exact.astype(BF16)
    return jnp.dot(hi, b, preferred_element_type=F32) + jnp.dot(mid, b, preferred_element_type=F32)


def _dot_hi(a, b):
    ah = a.astype(BF16)
    al = (a - ah.astype(F32)).astype(BF16)
    bh = b.astype(BF16)
    bl = (b - bh.astype(F32)).astype(BF16)
    return (jnp.dot(ah, bh, preferred_element_type=F32)
            + jnp.dot(al, bh, preferred_element_type=F32)
            + jnp.dot(ah, bl, preferred_element_type=F32))


def _sigmoid(x):
    return 1.0 / (1.0 + jnp.exp(-x))


def _in_proj_body(x_ref, g_ref, w_ref, cb_ref, sb_ref, pa_ref, pc_ref,
                  q1_ref, k1_ref, v1_ref, q4_ref, k4_ref, v4_ref, q16_ref, k16_ref, v16_ref, slab_s):
    tm = x_ref.shape[0]
    h = _rms(x_ref[...], g_ref[...])
    y = jnp.dot(h.astype(BF16), w_ref[...], preferred_element_type=F32)
    pa_ref[...] = y[:, :A_PAD]
    cb = jnp.concatenate([cb_ref[...]] * 2, axis=1)
    sb = jnp.concatenate([sb_ref[...]] * 2, axis=1)
    lane = lax.broadcasted_iota(jnp.int32, (1, B_WIDTH), 1)
    first = (lane % B_HEAD_DIM) < (B_ROT_DIM // 2)

    def rope(t):
        partner = jnp.where(first, pltpu.roll(t, B_WIDTH - B_ROT_DIM // 2, 1),
                            pltpu.roll(t, B_ROT_DIM // 2, 1))
        return t * cb + partner * sb

    o = A_PAD
    qkv = (rope(y[:, o:o + B_WIDTH]) * (B_HEAD_DIM ** -0.5),
           rope(y[:, o + B_WIDTH:o + 2 * B_WIDTH]),
           y[:, o + 2 * B_WIDTH:o + 3 * B_WIDTH])
    pc_ref[...] = y[:, o + 3 * B_WIDTH:]
    for i, (t, o_ref) in enumerate(zip(qkv, (q1_ref, k1_ref, v1_ref))):
        o_ref[...] = t.astype(BF16)
        slab_s[2 * i] = t[:, :128]
        slab_s[2 * i + 1] = t[:, 128:]
    for dil, outs in ((4, (q4_ref, k4_ref, v4_ref)), (16, (q16_ref, k16_ref, v16_ref))):
        gs = tm // dil
        for i, o_ref in enumerate(outs):
            for r in range(dil):
                rows = pl.ds(r, gs, stride=dil)
                o_ref[r * gs:(r + 1) * gs, :] = jnp.concatenate(
                    [slab_s[2 * i, rows, :], slab_s[2 * i + 1, rows, :]], axis=1).astype(BF16)


def _in_proj(x2, g, w, cb, sb, tm):
    n = x2.shape[0]
    row = lambda i: (i, 0)
    rb = lambda wd: pl.BlockSpec((tm, wd), row)
    res = lambda a: pl.BlockSpec(a.shape, lambda i: (0,) * a.ndim, pipeline_mode=pl.Buffered(1))
    bsd = jax.ShapeDtypeStruct((n, B_WIDTH), BF16)
    return pl.pallas_call(
        _in_proj_body,
        grid=(n // tm,),
        in_specs=[rb(D_MODEL), res(g), res(w), rb(128), rb(128)],
        out_specs=[rb(A_PAD), rb(C_PAD)] + [rb(B_WIDTH)] * 9,
        out_shape=[jax.ShapeDtypeStruct((n, A_PAD), F32), jax.ShapeDtypeStruct((n, C_PAD), F32)]
                  + [bsd] * 9,
        scratch_shapes=[pltpu.VMEM((6, tm, 128), F32)],
        compiler_params=_cparams(("parallel",)),
        name="in_proj",
    )(x2, g, w, cb, sb)


def _rwkv_body(pa_ref, mu_ref, w0_ref, dup_ref, a0_ref, iup_ref, gup_ref, kk_ref, ka_ref,
               rk_ref, lnw_ref, lnb_ref, out_ref, prev_s, h_s, *, ts):
    T = A_CHUNK
    HT = A_HEADS * T
    W = A_WIDTH
    C = ts // T

    @pl.when(pl.program_id(1) == 0)
    def _():
        prev_s[...] = jnp.zeros_like(prev_s)
        h_s[...] = jnp.zeros_like(h_s)

    lane_w = lax.broadcasted_iota(jnp.int32, (W, W), 1)
    row_w = lax.broadcasted_iota(jnp.int32, (W, W), 0)
    ebd = jnp.where(lane_w // A_HEAD_DIM == row_w // A_HEAD_DIM, 1.0, 0.0).astype(F32)

    pa = pa_ref[...]
    rid = lax.broadcasted_iota(jnp.int32, (ts, 1), 0)
    shifted = jnp.where(rid == 0, prev_s[0:1, :], pltpu.roll(pa, 1, 0))
    prev_s[0:1, :] = pa[ts - 1:ts, :]
    pf = pa + (shifted - pa) * mu_ref[...]
    r = pf[:, 0:W]
    k = pf[:, W:2 * W]
    v = pf[:, 2 * W:3 * W]
    xwa = pf[:, 3 * W:3 * W + 128]
    xg = pf[:, 3 * W + 128:]
    dl = _dot_hi(jnp.tanh(xwa), dup_ref[...])
    z = -(w0_ref[...] + dl)
    softplus = jnp.maximum(z, 0.0) + jnp.log(1.0 + jnp.exp(-jnp.abs(z)))
    w_log = -softplus - 0.5
    lw = -jnp.exp(w_log)
    a = _sigmoid(a0_ref[...] + _dot_hi(xwa, iup_ref[...]))
    g = _dot(_sigmoid(xg), gup_ref[...])
    kkv = k * kk_ref[...]
    ss = _dot_x2(kkv * kkv, ebd)
    kn = kkv / jnp.maximum(jnp.sqrt(ss), 1e-12)
    k2 = k * (1.0 + (a - 1.0) * ka_ref[...])
    bon = _dot_x2(r * k2 * rk_ref[...], ebd) * v
    b = kn * a

    rr = lax.broadcasted_iota(jnp.int32, (HT, HT), 0)
    cc = lax.broadcasted_iota(jnp.int32, (HT, HT), 1)
    strict = rr > cc
    lower = rr >= cc
    eye = rr == cc
    hmask = (rr // T) == (cc // A_HEAD_DIM)
    tr = lax.broadcasted_iota(jnp.int32, (T, T), 0)
    tc = lax.broadcasted_iota(jnp.int32, (T, T), 1)
    ltri = jnp.where(tr >= tc, 1.0, 0.0).astype(BF16)

    def stack(x):
        return jnp.where(hmask, jnp.concatenate([x] * A_HEADS, axis=0), 0.0).astype(BF16)

    def chunks(x):
        return [x[c * T:(c + 1) * T, :] for c in range(C)]

    def each(fn, *lists):
        return [fn(*xs) for xs in zip(*lists)]

    rc, kc, vc, knc, bc, lwc = (chunks(t) for t in (r, k2, v, kn, b, lw))
    cum = each(lambda x: _cumsum_rows(x, ltri), lwc)
    e_in = each(jnp.exp, cum)
    rs = each(lambda x, e: stack(x * e), rc, e_in)
    ks_ = each(lambda x, cu, l: stack(x * jnp.exp(cu - l)), knc, cum, lwc)
    e_out = each(lambda cu: jnp.exp(-cu), cum)
    bh = each(lambda x, e: stack(x * e), bc, e_out)
    kh = each(lambda x, e: stack(x * e), kc, e_out)
    e_end = each(lambda cu: jnp.exp(cu[T - 1:T, :] - cu), cum)
    bg = each(lambda x, e: stack(x * e), bc, e_end)
    kg = each(lambda x, e: stack(x * e), kc, e_end)
    vs = each(stack, vc)

    a4 = each(lambda k_, r_, b_, kh_: _dot_nt(jnp.concatenate([k_, r_], axis=0),
                                              jnp.concatenate([b_, kh_], axis=0)), ks_, rs, bh, kh)
    a_ab = each(lambda t: jnp.where(strict, t[:HT, :HT], 0.0), a4)
    a_akrk = each(lambda t: jnp.concatenate([jnp.where(strict, t[:HT, HT:], 0.0),
                                             jnp.where(lower, t[HT:, HT:], 0.0)], axis=0), a4)
    a_rb = each(lambda t: jnp.where(lower, t[HT:, :HT], 0.0), a4)

    m = each(lambda t: jnp.where(eye, 1.0, 0.0) - jnp.where((rr // 2 == cc // 2), t, 0.0), a_ab)
    s = 2
    while s < T:
        blk = (rr // (2 * s) == cc // (2 * s)) & ((rr // s) % 2 == 1) & ((cc // s) % 2 == 0)
        cm = each(lambda t, m_: _dot(jnp.where(blk, t, 0.0), m_), a_ab, m)
        m = each(lambda m_, t: m_ - _dot(m_, t), m, cm)
        s *= 2

    x1 = each(_dot, a_akrk, vs)
    pq = each(lambda m_, k_, x: _dot(m_, jnp.concatenate([k_, x[:HT].astype(BF16)], axis=1)),
              m, ks_, x1)
    x2 = each(_dot, a_rb, pq)
    rp = each(lambda r_, x: r_.astype(F32) - x[:, :W], rs, x2)
    y0 = each(lambda x1_, x: x1_[HT:] - x[:, W:], x1, x2)
    gd = each(_dot_tn, bg, pq)
    kv = each(_dot_tn, kg, vs)
    gmat = each(lambda e, t: jnp.where(eye, e[T - 1:T, :], 0.0) - t[:, :W], e_in, gd)
    dmat = each(lambda kv_, t: kv_ - t[:, W:], kv, gd)

    h = h_s[...]
    ys = []
    for c in range(C):
        ys.append(_dot(rp[c], h) + y0[c])
        h = _dot(gmat[c], h) + dmat[c]
    h_s[...] = h
    y = jnp.concatenate([t[0:T] + t[T:2 * T] + t[2 * T:3 * T] + t[3 * T:4 * T] for t in ys], axis=0)

    inv_n = 1.0 / A_HEAD_DIM
    mean = _dot_x2(y, ebd) * inv_n
    yc = y - mean
    var = _dot_x2(yc * yc, ebd) * inv_n
    yn = yc * lax.rsqrt(var + A_GN_EPS) * lnw_ref[...] + lnb_ref[...]
    out_ref[...] = (yn + bon) * g


def _cumsum_rows(x, ltri):
    hi, mid, lo = _split3(x)
    return (jnp.dot(ltri, hi, preferred_element_type=F32)
            + jnp.dot(ltri, mid, preferred_element_type=F32)
            + jnp.dot(ltri, lo, preferred_element_type=F32))


def _rwkv(pa, p, bsz, seq, ts):
    nt = seq // ts
    row = lambda b, i: (b * nt + i, 0)
    const = lambda b, i: (0, 0)
    vec = lambda w: pl.BlockSpec((1, w), const)
    return pl.pallas_call(
        functools.partial(_rwkv_body, ts=ts),
        grid=(bsz, nt),
        in_specs=[pl.BlockSpec((ts, A_PAD), row), vec(A_PAD), vec(A_WIDTH),
                  pl.BlockSpec((128, A_WIDTH), const), vec(A_WIDTH),
                  pl.BlockSpec((128, A_WIDTH), const),
                  pl.BlockSpec((A_WIDTH, A_WIDTH), const),
                  vec(A_WIDTH), vec(A_WIDTH), vec(A_WIDTH), vec(A_WIDTH), vec(A_WIDTH)],
        out_specs=pl.BlockSpec((ts, A_WIDTH), row),
        out_shape=jax.ShapeDtypeStruct((bsz * seq, A_WIDTH), F32),
        scratch_shapes=[pltpu.VMEM((8, A_PAD), F32), pltpu.VMEM((A_WIDTH, A_WIDTH), F32)],
        compiler_params=_cparams(("parallel", "arbitrary")),
        name="rwkv7",
    )(pa, p["mu"], p["w0"], p["dup"], p["a0"], p["iup"], p["gup"], p["k_k"], p["k_a"],
      p["r_k"], p["ln_w"], p["ln_b"])


def _dilated_body(q_ref, k_ref, v_ref, o_ref, l_ref, *, nb, gs):
    Q = B_BLOCK
    HQ = B_HEADS * Q
    W = B_WIDTH
    rr = lax.broadcasted_iota(jnp.int32, (HQ, W), 0)
    cc = lax.broadcasted_iota(jnp.int32, (HQ, W), 1)
    hsel = jnp.where((rr // Q) == (cc // B_HEAD_DIM), 1.0, 0.0).astype(BF16)
    qi = rr % Q
    dist = qi + Q - cc
    band2 = (dist >= 0) & (dist <= Q)
    lane_h = lax.broadcasted_iota(jnp.int32, (Q, W), 1) // B_HEAD_DIM

    def where(n):
        if gs >= Q:
            per = gs // Q
            return n // per, pl.ds(pl.multiple_of((n % per) * Q, Q), Q)
        per = Q // gs
        return pl.ds(pl.multiple_of(n * per, per), per), slice(None)

    def load(ref, n):
        g, rows = where(n)
        return ref[g, rows, :].reshape(Q, W)

    def store(ref, n, val):
        g, rows = where(n)
        ref[g, rows, :] = val.reshape(ref[g, rows, :].shape)

    def each(fn, *lists):
        return [fn(*xs) for xs in zip(*lists)]

    def attend(ns):
        first = [n == 0 for n in ns]
        prev = [jnp.maximum(n - 1, 0) for n in ns]
        qs = [jnp.concatenate([load(q_ref, n)] * B_HEADS, axis=0) * hsel for n in ns]
        kw = [jnp.concatenate([load(k_ref, p), load(k_ref, n)], axis=0) for p, n in zip(prev, ns)]
        vw = [jnp.concatenate([load(v_ref, p), load(v_ref, n)], axis=0) for p, n in zip(prev, ns)]
        s = each(lambda a, b: lax.dot_general(a, b, (((1,), (1,)), ((), ())),
                                              preferred_element_type=F32), qs, kw)
        s = each(lambda t, f: jnp.where(band2 & ((cc >= Q) | jnp.logical_not(f)), t, NEG_BIG), s, first)
        m = each(lambda t: jnp.max(t, axis=-1, keepdims=True), s)
        p = each(lambda t, m_: jnp.exp(t - m_), s, m)
        l = each(lambda t: jnp.sum(t, axis=-1, keepdims=True), p)
        o = each(lambda p_, v_, l_: jnp.dot(p_.astype(BF16), v_, preferred_element_type=F32) / l_,
                 p, vw, l)
        lse = each(lambda m_, l_: jnp.broadcast_to(m_ + jnp.log(l_), (HQ, W)), m, l)
        for n, o_, lse_ in zip(ns, o, lse):
            out = o_[0:Q]
            lout = lse_[0:Q]
            for h in range(1, B_HEADS):
                out = jnp.where(lane_h == h, o_[h * Q:(h + 1) * Q], out)
                lout = jnp.where(lane_h == h, lse_[h * Q:(h + 1) * Q], lout)
            store(o_ref, n, out.astype(o_ref.dtype))
            store(l_ref, n, lout)

    unroll = min(DIL_UNROLL, nb)

    def body(t, carry):
        attend([t * unroll + i for i in range(unroll)])
        return carry

    lax.fori_loop(0, nb // unroll, body, 0)


def _dilated_branch(q, k, v, bsz, seq, dil, tm):
    gs = tm // dil
    nt = seq // tm
    view = lambda t: t.reshape(bsz, nt, dil, gs, B_WIDTH)
    spec = pl.BlockSpec((None, nt, None, gs, B_WIDTH), lambda b, r: (b, 0, r, 0, 0))
    o, l = pl.pallas_call(
        functools.partial(_dilated_body, nb=seq // dil // B_BLOCK, gs=gs),
        grid=(bsz, dil),
        in_specs=[spec, spec, spec],
        out_specs=[spec, spec],
        out_shape=[jax.ShapeDtypeStruct((bsz, nt, dil, gs, B_WIDTH), BF16),
                   jax.ShapeDtypeStruct((bsz, nt, dil, gs, B_WIDTH), F32)],
        compiler_params=_cparams(("parallel", "parallel")),
        name="dilated_d%d" % dil,
    )(view(q), view(k), view(v))
    return o.reshape(bsz * seq, B_WIDTH), l.reshape(bsz * seq, B_WIDTH)


def _mla_up_body(pc_ref, qg_ref, kvg_ref, wq_ref, wkv_ref, cc_ref, sc_ref, q_ref, k_ref, v_ref):
    pc = pc_ref[...]
    cq = _rms(pc[:, :C_Q_LORA], qg_ref[...])
    ckv = _rms(pc[:, C_Q_LORA:C_Q_LORA + C_KV_LORA], kvg_ref[...])
    kr = pc[:, C_Q_LORA + C_KV_LORA:]
    scale = (C_NOPE_DIM + C_ROPE_DIM) ** -0.5
    q = jnp.dot(cq.astype(BF16), wq_ref[...], preferred_element_type=F32) * scale
    kv = jnp.dot(ckv.astype(BF16), wkv_ref[...], preferred_element_type=F32)
    cc = cc_ref[...]
    sc = sc_ref[...]

    def rope(t):
        return t * cc + pltpu.roll(t, 64, 1) * sc

    krr = rope(kr).astype(BF16)
    for h in range(C_HEADS):
        o = h * C_QK_PAD
        q_ref[0, h, :, 0:C_NOPE_DIM] = q[:, o:o + C_NOPE_DIM].astype(BF16)
        q_ref[0, h, :, C_NOPE_DIM:] = rope(q[:, o + C_NOPE_DIM:o + C_QK_PAD]).astype(BF16)
        k_ref[0, h, :, 0:C_NOPE_DIM] = kv[:, h * C_NOPE_DIM:(h + 1) * C_NOPE_DIM].astype(BF16)
        k_ref[0, h, :, C_NOPE_DIM:] = krr
        vo = C_HEADS * C_NOPE_DIM + h * C_V_DIM
        v_ref[0, h, :, 0:C_V_DIM] = kv[:, vo:vo + C_V_DIM].astype(BF16)
        v_ref[0, h, :, C_V_DIM:] = jnp.ones((kv.shape[0], C_V_DIM), BF16)


def _mla_up(pc, qg, kvg, wq, wkv, cc, sc, bsz, seq, ts):
    nt = seq // ts
    row = lambda b, i: (b * nt + i, 0)
    const = lambda b, i: (0, 0)
    hm = lambda b, i: (b, 0, i, 0)
    return pl.pallas_call(
        _mla_up_body,
        grid=(bsz, nt),
        in_specs=[pl.BlockSpec((ts, C_PAD), row),
                  pl.BlockSpec((1, C_Q_LORA), const),
                  pl.BlockSpec((1, C_KV_LORA), const),
                  pl.BlockSpec((C_Q_LORA, C_HEADS * C_QK_PAD), const),
                  pl.BlockSpec((C_KV_LORA, C_HEADS * (C_NOPE_DIM + C_V_DIM)), const),
                  pl.BlockSpec((ts, 128), row),
                  pl.BlockSpec((ts, 128), row)],
        out_specs=[pl.BlockSpec((1, C_HEADS, ts, C_QK_PAD), hm),
                   pl.BlockSpec((1, C_HEADS, ts, C_QK_PAD), hm),
                   pl.BlockSpec((1, C_HEADS, ts, 2 * C_V_DIM), hm)],
        out_shape=[jax.ShapeDtypeStruct((bsz, C_HEADS, seq, C_QK_PAD), BF16),
                   jax.ShapeDtypeStruct((bsz, C_HEADS, seq, C_QK_PAD), BF16),
                   jax.ShapeDtypeStruct((bsz, C_HEADS, seq, 2 * C_V_DIM), BF16)],
        compiler_params=_cparams(("parallel", "parallel")),
        name="mla_up",
    )(pc, qg, kvg, wq, wkv, cc, sc)


def _flash_body(q_ref, k_ref, v_ref, o_ref, m_s, l_s, acc_s, sa_s, sb_s, *, tq):
    qi = pl.program_id(2)
    q = q_ref[0, 0]
    rr = lax.broadcasted_iota(jnp.int32, (tq, tq), 0)
    cc = lax.broadcasted_iota(jnp.int32, (tq, tq), 1)
    causal = cc <= rr
    reps = tq // 128
    m_s[...] = jnp.full_like(m_s, NEG_BIG)
    l_s[...] = jnp.zeros_like(l_s)
    acc_s[...] = jnp.zeros_like(acc_s)

    def scores(j, s_ref):
        k0 = pl.multiple_of(j * tq, tq)
        kb = k_ref[0, 0, pl.ds(k0, tq), :]
        s_ref[...] = lax.dot_general(q, kb, (((1,), (1,)), ((), ())), preferred_element_type=F32)

    def update(j, s_ref, masked):
        k0 = pl.multiple_of(j * tq, tq)
        vb = v_ref[0, 0, pl.ds(k0, tq), :]
        s = s_ref[...]
        if masked:
            s = jnp.where(causal, s, NEG_BIG)
        m_prev = m_s[...]
        m_new = jnp.maximum(m_prev, jnp.max(s, axis=-1, keepdims=True))
        alpha = jnp.exp(m_prev - m_new)
        p = jnp.exp(s - jnp.concatenate([m_new] * reps, axis=1))
        pv = jnp.dot(p.astype(BF16), vb, preferred_element_type=F32)
        l_s[...] = alpha * l_s[...] + pv[:, C_V_DIM:]
        acc_s[...] = alpha * acc_s[...] + pv[:, :C_V_DIM]
        m_s[...] = m_new

    scores(0, sa_s)

    def pair(t, carry):
        scores(2 * t + 1, sb_s)
        update(2 * t, sa_s, False)
        scores(2 * t + 2, sa_s)
        update(2 * t + 1, sb_s, False)
        return carry

    lax.fori_loop(0, qi // 2, pair, 0)

    @pl.when(qi % 2 == 0)
    def _():
        update(qi, sa_s, True)

    @pl.when(qi % 2 == 1)
    def _():
        scores(qi, sb_s)
        update(qi - 1, sa_s, False)
        update(qi, sb_s, True)

    o_ref[0] = acc_s[...] / l_s[...]


def _flash(q, k, v, bsz, seq, tq):
    nq = seq // tq
    stat = lambda: pltpu.VMEM((tq, C_V_DIM), F32)
    return pl.pallas_call(
        functools.partial(_flash_body, tq=tq),
        grid=(bsz, C_HEADS, nq),
        in_specs=[pl.BlockSpec((1, 1, tq, C_QK_PAD), lambda b, h, i: (b, h, i, 0)),
                  pl.BlockSpec((1, 1, seq, C_QK_PAD), lambda b, h, i: (b, h, 0, 0)),
                  pl.BlockSpec((1, 1, seq, 2 * C_V_DIM), lambda b, h, i: (b, h, 0, 0))],
        out_specs=pl.BlockSpec((1, tq, C_V_DIM), lambda b, h, i: (b, i, h)),
        out_shape=jax.ShapeDtypeStruct((bsz, seq, C_WIDTH), F32),
        scratch_shapes=[stat(), stat(), stat(), pltpu.VMEM((tq, tq), F32), pltpu.VMEM((tq, tq), F32)],
        compiler_params=_cparams(("parallel", "parallel", "arbitrary")),
        name="mla_flash",
    )(q, k, v)


def _ffn_body(x_ref, ya_ref, o1_ref, o2_ref, o3_ref, l1_ref, l2_ref, l3_ref, yc_ref,
              wo_ref, g_ref, wg_ref, wu_ref, wd_ref, fg_ref, out_ref, h_s, slab_s, *, final):
    tm = x_ref.shape[0]

    def natural(dil, o_ref, l_ref):
        gs = tm // dil
        for r in range(dil):
            rows = pl.ds(r, gs, stride=dil)
            ov = o_ref[r * gs:(r + 1) * gs, :].astype(F32)
            lv = l_ref[r * gs:(r + 1) * gs, :]
            slab_s[0, rows, :] = ov[:, :128]
            slab_s[1, rows, :] = ov[:, 128:]
            slab_s[2, rows, :] = lv[:, :128]
            slab_s[3, rows, :] = lv[:, 128:]
        return (jnp.concatenate([slab_s[0], slab_s[1]], axis=1),
                jnp.concatenate([slab_s[2], slab_s[3]], axis=1))

    mix = (jnp.dot(ya_ref[...].astype(BF16), wo_ref[0:A_WIDTH, :], preferred_element_type=F32)
           + jnp.dot(yc_ref[...].astype(BF16), wo_ref[A_WIDTH + B_WIDTH:, :],
                     preferred_element_type=F32))
    o1, l1 = o1_ref[...].astype(F32), l1_ref[...]
    o2, l2 = natural(B_PATTERNS[1][1], o2_ref, l2_ref)
    o3, l3 = natural(B_PATTERNS[2][1], o3_ref, l3_ref)
    mx = jnp.maximum(jnp.maximum(l1, l2), l3)
    e1, e2, e3 = jnp.exp(l1 - mx), jnp.exp(l2 - mx), jnp.exp(l3 - mx)
    yb = (e1 * o1 + e2 * o2 + e3 * o3) / (e1 + e2 + e3)
    mix = mix + jnp.dot(yb.astype(BF16), wo_ref[A_WIDTH:A_WIDTH + B_WIDTH, :],
                        preferred_element_type=F32)
    xn = x_ref[...] + mix
    out_ref[...] = xn
    h_s[...] = _rms(xn, g_ref[...]).astype(BF16)

    def step(j, carry):
        h = h_s[...]
        gt = jnp.dot(h, wg_ref[j], preferred_element_type=F32)
        up = jnp.dot(h, wu_ref[j], preferred_element_type=F32)
        act = gt * _sigmoid(gt) * up
        out_ref[...] += jnp.dot(act.astype(BF16), wd_ref[j], preferred_element_type=F32)
        return carry

    lax.fori_loop(0, wg_ref.shape[0], step, 0, unroll=True)
    if final:
        out_ref[...] = _rms(out_ref[...], fg_ref[...])


def _ffn(x2, ya, ob, lb, yc, wo, g, wg, wu, wd, fg, tm, final):
    n = x2.shape[0]
    row = lambda i: (i, 0)
    rb = lambda w: pl.BlockSpec((tm, w), row)
    res = lambda a: pl.BlockSpec(a.shape, lambda i: (0,) * a.ndim, pipeline_mode=pl.Buffered(1))
    return pl.pallas_call(
        functools.partial(_ffn_body, final=final),
        grid=(n // tm,),
        in_specs=[rb(D_MODEL), rb(A_WIDTH), rb(B_WIDTH), rb(B_WIDTH), rb(B_WIDTH),
                  rb(B_WIDTH), rb(B_WIDTH), rb(B_WIDTH), rb(C_WIDTH),
                  res(wo), res(g), res(wg), res(wu), res(wd), res(fg)],
        out_specs=pl.BlockSpec((tm, D_MODEL), row),
        out_shape=jax.ShapeDtypeStruct((n, D_MODEL), F32),
        scratch_shapes=[pltpu.VMEM((tm, D_MODEL), BF16), pltpu.VMEM((4, tm, 128), F32)],
        compiler_params=_cparams(("parallel",)),
        name="out_proj_ffn",
    )(x2, ya, ob[0], ob[1], ob[2], lb[0], lb[1], lb[2], yc, wo, g, wg, wu, wd, fg)


def _pad_cols(w, width):
    return jnp.pad(w, ((0, 0), (0, width - w.shape[1])))


def _rope_layout_cols(w):
    z = jnp.zeros(w.shape[:-1] + (32,), w.dtype)
    return jnp.concatenate([w[..., :32], z, w[..., 32:], z], axis=-1)


def _layer_params(l, w_in, a_mu, a_w0, a_decay_up, a_a0, a_iclr_up, a_gate_up, a_k_k, a_k_a,
                  a_r_k, a_ln_w, a_ln_b, c_w_uq, c_w_ukv):
    wi = w_in[l]
    oa = A_PROJ
    ob = oa + 3 * B_WIDTH
    w_all = jnp.concatenate([
        _pad_cols(wi[:, :oa], A_PAD),
        wi[:, oa:ob],
        wi[:, ob:ob + C_Q_LORA + C_KV_LORA],
        _rope_layout_cols(wi[:, ob + C_Q_LORA + C_KV_LORA:]),
    ], axis=1).astype(BF16)
    z64 = jnp.zeros((64, A_WIDTH), F32)
    rw = lambda t: t.reshape(1, -1)
    wq = c_w_uq[l].reshape(C_Q_LORA, C_HEADS, C_NOPE_DIM + C_ROPE_DIM)
    wq = jnp.concatenate([wq[..., :C_NOPE_DIM], _rope_layout_cols(wq[..., C_NOPE_DIM:])], axis=-1)
    wkv = c_w_ukv[l].reshape(C_KV_LORA, C_HEADS, C_NOPE_DIM + C_V_DIM)
    wkv = jnp.concatenate([wkv[..., :C_NOPE_DIM].reshape(C_KV_LORA, -1),
                           wkv[..., C_NOPE_DIM:].reshape(C_KV_LORA, -1)], axis=1)
    return dict(
        w_all=w_all,
        mu=_pad_cols(rw(a_mu[l]), A_PAD), w0=rw(a_w0[l]),
        dup=jnp.concatenate([a_decay_up[l], z64], axis=0),
        a0=rw(a_a0[l]),
        iup=jnp.concatenate([z64, a_iclr_up[l]], axis=0),
        gup=jnp.pad(a_gate_up[l], ((0, A_WIDTH - A_GATE_LORA), (0, 0))),
        k_k=rw(a_k_k[l]), k_a=rw(a_k_a[l]), r_k=rw(a_r_k[l]),
        ln_w=rw(a_ln_w[l]), ln_b=rw(a_ln_b[l]),
        wq=wq.reshape(C_Q_LORA, C_HEADS * C_QK_PAD).astype(BF16),
        wkv=wkv.astype(BF16),
    )


def _rope_tables(positions):
    n = positions.size
    pos = positions.reshape(n, 1).astype(F32)

    def tables(dim):
        inv_freq = 1.0 / (ROPE_THETA ** (jnp.arange(0, dim, 2, dtype=F32) / dim))
        ang = pos * inv_freq
        return jnp.cos(ang), jnp.sin(ang)

    cb, sb = tables(B_ROT_DIM)
    one = jnp.ones((n, B_HEAD_DIM - B_ROT_DIM), F32)
    cb = jnp.tile(jnp.concatenate([cb, cb, one], axis=1), (1, 128 // B_HEAD_DIM))
    sb = jnp.tile(jnp.concatenate([-sb, sb, 0.0 * one], axis=1), (1, 128 // B_HEAD_DIM))
    cc, sc = tables(C_ROPE_DIM)
    z = jnp.zeros_like(cc)
    cc = jnp.concatenate([cc, z, cc, z], axis=1)
    sc = jnp.concatenate([-sc, z, sc, z], axis=1)
    return cb, sb, cc, sc


def kernel(x, positions, attn_norm_g, w_in, a_mu, a_w0, a_decay_up, a_a0, a_iclr_up, a_gate_up, a_k_k, a_k_a, a_r_k, a_ln_w, a_ln_b, c_q_norm_g, c_kv_norm_g, c_w_uq, c_w_ukv, w_out, ffn_norm_g, ffn_w_gate, ffn_w_up, ffn_w_down, final_norm_g):
    bsz, seq, _ = x.shape
    n = bsz * seq
    depth = w_in.shape[0]
    assert seq % max(w for w, _ in B_PATTERNS) == 0
    cb, sb, cc, sc = _rope_tables(positions)
    x2 = x.reshape(n, D_MODEL)
    tm = min(512, seq)
    for l in range(depth):
        p = _layer_params(l, w_in, a_mu, a_w0, a_decay_up, a_a0, a_iclr_up, a_gate_up, a_k_k,
                          a_k_a, a_r_k, a_ln_w, a_ln_b, c_w_uq, c_w_ukv)
        pa, pc, *qkv = _in_proj(x2, attn_norm_g[l].reshape(1, -1), p["w_all"], cb, sb, tm)
        ya = _rwkv(pa, p, bsz, seq, min(RWKV_TILE, seq))
        ob, lb = zip(*[_dilated_branch(*qkv[3 * i:3 * i + 3], bsz, seq, d, tm)
                       for i, (_, d) in enumerate(B_PATTERNS)])
        qc, kc, vc = _mla_up(pc, c_q_norm_g[l].reshape(1, -1), c_kv_norm_g[l].reshape(1, -1),
                             p["wq"], p["wkv"], cc, sc, bsz, seq, tm)
        yc = _flash(qc, kc, vc, bsz, seq, min(512, seq)).reshape(n, C_WIDTH)
        nf = D_FF // FF_CHUNK
        ffw = lambda w: w.astype(BF16).reshape(D_MODEL, nf, FF_CHUNK).transpose(1, 0, 2)
        x2 = _ffn(x2, ya, ob, lb, yc, w_out[l].astype(BF16), ffn_norm_g[l].reshape(1, -1),
                  ffw(ffn_w_gate[l]), ffw(ffn_w_up[l]),
                  ffn_w_down[l].astype(BF16).reshape(nf, FF_CHUNK, D_MODEL),
                  final_norm_g.reshape(1, -1), tm, l == depth - 1)
    return x2.reshape(bsz, seq, D_MODEL)
```

```python
import functools

import jax
import jax.numpy as jnp
from jax import lax
from jax.experimental import pallas as pl
from jax.experimental.pallas import tpu as pltpu

F32 = jnp.float32
BF16 = jnp.bfloat16

D_MODEL = 1024
NORM_EPS = 1e-6
ROPE_THETA = 500000.0

A_HEADS = 4
A_HEAD_DIM = 64
A_WIDTH = 256
A_DECAY_LORA = 64
A_ICLR_LORA = 64
A_GATE_LORA = 160
A_GN_EPS = 64e-5
A_PROJ = 3 * A_WIDTH + A_DECAY_LORA + A_ICLR_LORA + A_GATE_LORA
A_PAD = 1152
A_CHUNK = 64
RWKV_TILE = 512

B_HEADS = 4
B_HEAD_DIM = 64
B_WIDTH = 256
B_ROT_DIM = 16
B_PATTERNS = ((128, 1), (512, 4), (2048, 16))
B_BLOCK = 128
DIL_UNROLL = 2

C_HEADS = 4
C_NOPE_DIM = 128
C_ROPE_DIM = 64
C_V_DIM = 128
C_Q_LORA = 256
C_KV_LORA = 128
C_WIDTH = 512
C_QK_PAD = 256
C_PAD = 512

MIX_WIDTH = 1024
P_PAD = A_PAD + 3 * B_WIDTH + C_PAD
D_FF = 2816
FF_CHUNK = 256

VMEM_LIMIT = 56 * 1024 * 1024
NEG_BIG = -1e30


def _cparams(sem):
    return pltpu.CompilerParams(dimension_semantics=sem, vmem_limit_bytes=VMEM_LIMIT)


def _rms(x, g):
    return x * lax.rsqrt(jnp.mean(x * x, axis=-1, keepdims=True) + NORM_EPS) * g


def _dot(a, b):
    return jnp.dot(a.astype(BF16), b.astype(BF16), preferred_element_type=F32)


def _dot_nt(a, b):
    return lax.dot_general(a.astype(BF16), b.astype(BF16), (((1,), (1,)), ((), ())),
                           preferred_element_type=F32)


def _dot_tn(a, b):
    return lax.dot_general(a.astype(BF16), b.astype(BF16), (((0,), (0,)), ((), ())),
                           preferred_element_type=F32)


def _split3(x):
    hi = x.astype(BF16)
    r1 = x - hi.astype(F32)
    mid = r1.astype(BF16)
    lo = (r1 - mid.astype(F32)).astype(BF16)
    return hi, mid, lo


def _dot_hi(a, b):
    ah = a.astype(BF16)
    al = (a - ah.astype(F32)).astype(BF16)
    bh = b.astype(BF16)
    bl = (b - bh.astype(F32)).astype(BF16)
    return (jnp.dot(ah, bh, preferred_element_type=F32)
            + jnp.dot(al, bh, preferred_element_type=F32)
            + jnp.dot(ah, bl, preferred_element_type=F32))


def _sigmoid(x):
    return 1.0 / (1.0 + jnp.exp(-x))


def _in_proj_body(x_ref, g_ref, w_ref, cb_ref, sb_ref, qg_ref, kvg_ref, wq_ref, wkv_ref, cc_ref, sc_ref,
                  pa_ref, q1_ref, k1_ref, v1_ref, q4_ref, k4_ref, v4_ref, q16_ref, k16_ref, v16_ref,
                  qc_ref, kc_ref, vc_ref, slab_s):
    tm = x_ref.shape[0]
    h = _rms(x_ref[...], g_ref[...])
    y = jnp.dot(h.astype(BF16), w_ref[...], preferred_element_type=F32)
    pa_ref[...] = y[:, :A_PAD]
    cb = jnp.concatenate([cb_ref[...]] * 2, axis=1)
    sb = jnp.concatenate([sb_ref[...]] * 2, axis=1)
    lane = lax.broadcasted_iota(jnp.int32, (1, B_WIDTH), 1)
    first = (lane % B_HEAD_DIM) < (B_ROT_DIM // 2)

    def rope(t):
        partner = jnp.where(first, pltpu.roll(t, B_WIDTH - B_ROT_DIM // 2, 1),
                            pltpu.roll(t, B_ROT_DIM // 2, 1))
        return t * cb + partner * sb

    o = A_PAD
    qkv = (rope(y[:, o:o + B_WIDTH]) * (B_HEAD_DIM ** -0.5),
           rope(y[:, o + B_WIDTH:o + 2 * B_WIDTH]),
           y[:, o + 2 * B_WIDTH:o + 3 * B_WIDTH])
    _mla_up(y[:, o + 3 * B_WIDTH:], qg_ref, kvg_ref, wq_ref, wkv_ref, cc_ref, sc_ref,
            qc_ref, kc_ref, vc_ref)
    for i, (t, o_ref) in enumerate(zip(qkv, (q1_ref, k1_ref, v1_ref))):
        o_ref[...] = t.astype(BF16)
        slab_s[2 * i] = t[:, :128]
        slab_s[2 * i + 1] = t[:, 128:]
    for dil, outs in ((4, (q4_ref, k4_ref, v4_ref)), (16, (q16_ref, k16_ref, v16_ref))):
        gs = tm // dil
        for i, o_ref in enumerate(outs):
            for r in range(dil):
                rows = pl.ds(r, gs, stride=dil)
                o_ref[r * gs:(r + 1) * gs, :] = jnp.concatenate(
                    [slab_s[2 * i, rows, :], slab_s[2 * i + 1, rows, :]], axis=1).astype(BF16)


def _mla_up(pc, qg_ref, kvg_ref, wq_ref, wkv_ref, cc_ref, sc_ref, q_ref, k_ref, v_ref):
    cq = _rms(pc[:, :C_Q_LORA], qg_ref[...])
    ckv = _rms(pc[:, C_Q_LORA:C_Q_LORA + C_KV_LORA], kvg_ref[...])
    kr = pc[:, C_Q_LORA + C_KV_LORA:]
    scale = (C_NOPE_DIM + C_ROPE_DIM) ** -0.5
    q = jnp.dot(cq.astype(BF16), wq_ref[...], preferred_element_type=F32) * scale
    kv = jnp.dot(ckv.astype(BF16), wkv_ref[...], preferred_element_type=F32)
    cc = cc_ref[...]
    sc = sc_ref[...]

    def rope(t):
        return t * cc + pltpu.roll(t, 64, 1) * sc

    krr = rope(kr).astype(BF16)
    for h in range(C_HEADS):
        o = h * C_QK_PAD
        q_ref[0, h, :, 0:C_NOPE_DIM] = q[:, o:o + C_NOPE_DIM].astype(BF16)
        q_ref[0, h, :, C_NOPE_DIM:] = rope(q[:, o + C_NOPE_DIM:o + C_QK_PAD]).astype(BF16)
        k_ref[0, h, :, 0:C_NOPE_DIM] = kv[:, h * C_NOPE_DIM:(h + 1) * C_NOPE_DIM].astype(BF16)
        k_ref[0, h, :, C_NOPE_DIM:] = krr
        vo = C_HEADS * C_NOPE_DIM + h * C_V_DIM
        v_ref[0, h, :, 0:C_V_DIM] = kv[:, vo:vo + C_V_DIM].astype(BF16)
        v_ref[0, h, :, C_V_DIM:] = jnp.ones((kv.shape[0], C_V_DIM), BF16)


def _in_proj(x2, g, w, cb, sb, qg, kvg, wq, wkv, cc, sc, bsz, seq, tm):
    n = x2.shape[0]
    nt = seq // tm
    row = lambda i: (i, 0)
    rb = lambda wd: pl.BlockSpec((tm, wd), row)
    res = lambda a: pl.BlockSpec(a.shape, lambda i: (0,) * a.ndim, pipeline_mode=pl.Buffered(1))
    hm = lambda wd: pl.BlockSpec((1, C_HEADS, tm, wd), lambda i: (i // nt, 0, i % nt, 0))
    bsd = jax.ShapeDtypeStruct((n, B_WIDTH), BF16)
    hsd = lambda wd: jax.ShapeDtypeStruct((bsz, C_HEADS, seq, wd), BF16)
    return pl.pallas_call(
        _in_proj_body,
        grid=(n // tm,),
        in_specs=[rb(D_MODEL), res(g), res(w), rb(128), rb(128),
                  res(qg), res(kvg), res(wq), res(wkv), rb(128), rb(128)],
        out_specs=[rb(A_PAD)] + [rb(B_WIDTH)] * 9 + [hm(C_QK_PAD), hm(C_QK_PAD), hm(2 * C_V_DIM)],
        out_shape=[jax.ShapeDtypeStruct((n, A_PAD), F32)] + [bsd] * 9
                  + [hsd(C_QK_PAD), hsd(C_QK_PAD), hsd(2 * C_V_DIM)],
        scratch_shapes=[pltpu.VMEM((6, tm, 128), F32)],
        compiler_params=_cparams(("parallel",)),
        name="in_proj",
    )(x2, g, w, cb, sb, qg, kvg, wq, wkv, cc, sc)


def _rwkv_body(pa_ref, mu_ref, w0_ref, dup_ref, a0_ref, iup_ref, gup_ref, kk_ref, ka_ref,
               rk_ref, lnw_ref, lnb_ref, out_ref, prev_s, h_s, *, ts):
    T = A_CHUNK
    HT = A_HEADS * T
    W = A_WIDTH
    C = ts // T

    @pl.when(pl.program_id(1) == 0)
    def _():
        prev_s[...] = jnp.zeros_like(prev_s)
        h_s[...] = jnp.zeros_like(h_s)

    lane_w = lax.broadcasted_iota(jnp.int32, (W, W), 1)
    row_w = lax.broadcasted_iota(jnp.int32, (W, W), 0)
    ebd = jnp.where(lane_w // A_HEAD_DIM == row_w // A_HEAD_DIM, 1.0, 0.0).astype(F32)

    pa = pa_ref[...]
    rid = lax.broadcasted_iota(jnp.int32, (ts, 1), 0)
    shifted = jnp.where(rid == 0, prev_s[0:1, :], pltpu.roll(pa, 1, 0))
    prev_s[0:1, :] = pa[ts - 1:ts, :]
    pf = pa + (shifted - pa) * mu_ref[...]
    r = pf[:, 0:W]
    k = pf[:, W:2 * W]
    v = pf[:, 2 * W:3 * W]
    xwa = pf[:, 3 * W:3 * W + 128]
    xg = pf[:, 3 * W + 128:]
    dl = _dot_hi(jnp.tanh(xwa), dup_ref[...])
    z = -(w0_ref[...] + dl)
    softplus = jnp.maximum(z, 0.0) + jnp.log(1.0 + jnp.exp(-jnp.abs(z)))
    w_log = -softplus - 0.5
    lw = -jnp.exp(w_log)
    a = _sigmoid(a0_ref[...] + _dot(xwa, iup_ref[...]))
    g = _dot(_sigmoid(xg), gup_ref[...])
    kkv = k * kk_ref[...]
    ss = _dot(kkv * kkv, ebd)
    kn = kkv / jnp.maximum(jnp.sqrt(ss), 1e-12)
    k2 = k * (1.0 + (a - 1.0) * ka_ref[...])
    bon = _dot(r * k2 * rk_ref[...], ebd) * v
    b = kn * a

    rr = lax.broadcasted_iota(jnp.int32, (HT, HT), 0)
    cc = lax.broadcasted_iota(jnp.int32, (HT, HT), 1)
    strict = rr > cc
    lower = rr >= cc
    eye = rr == cc
    hmask = (rr // T) == (cc // A_HEAD_DIM)
    tr = lax.broadcasted_iota(jnp.int32, (T, T), 0)
    tc = lax.broadcasted_iota(jnp.int32, (T, T), 1)
    ltri = jnp.where(tr >= tc, 1.0, 0.0).astype(BF16)

    def stack(x):
        return jnp.where(hmask, jnp.concatenate([x] * A_HEADS, axis=0), 0.0).astype(BF16)

    def chunks(x):
        return [x[c * T:(c + 1) * T, :] for c in range(C)]

    def each(fn, *lists):
        return [fn(*xs) for xs in zip(*lists)]

    rc, kc, vc, knc, bc, lwc = (chunks(t) for t in (r, k2, v, kn, b, lw))
    cum = each(lambda x: _cumsum_rows(x, ltri), lwc)
    e_in = each(jnp.exp, cum)
    rs = each(lambda x, e: stack(x * e), rc, e_in)
    ks_ = each(lambda x, cu, l: stack(x * jnp.exp(cu - l)), knc, cum, lwc)
    e_out = each(lambda cu: jnp.exp(-cu), cum)
    bh = each(lambda x, e: stack(x * e), bc, e_out)
    kh = each(lambda x, e: stack(x * e), kc, e_out)
    e_end = each(lambda cu: jnp.exp(cu[T - 1:T, :] - cu), cum)
    bg = each(lambda x, e: stack(x * e), bc, e_end)
    kg = each(lambda x, e: stack(x * e), kc, e_end)
    vs = each(stack, vc)

    a4 = each(lambda k_, r_, b_, kh_: _dot_nt(jnp.concatenate([k_, r_], axis=0),
                                              jnp.concatenate([b_, kh_], axis=0)), ks_, rs, bh, kh)
    a_ab = each(lambda t: jnp.where(strict, t[:HT, :HT], 0.0), a4)
    a_akrk = each(lambda t: jnp.concatenate([jnp.where(strict, t[:HT, HT:], 0.0),
                                             jnp.where(lower, t[HT:, HT:], 0.0)], axis=0), a4)
    a_rb = each(lambda t: jnp.where(lower, t[HT:, :HT], 0.0), a4)

    m = each(lambda t: jnp.where(eye, 1.0, 0.0) - jnp.where((rr // 2 == cc // 2), t, 0.0), a_ab)
    s = 2
    while s < T:
        blk = (rr // (2 * s) == cc // (2 * s)) & ((rr // s) % 2 == 1) & ((cc // s) % 2 == 0)
        cm = each(lambda t, m_: _dot(jnp.where(blk, t, 0.0), m_), a_ab, m)
        m = each(lambda m_, t: m_ - _dot(m_, t), m, cm)
        s *= 2

    x1 = each(_dot, a_akrk, vs)
    pq = each(lambda m_, k_, x: _dot(m_, jnp.concatenate([k_, x[:HT].astype(BF16)], axis=1)),
              m, ks_, x1)
    x2 = each(_dot, a_rb, pq)
    rp = each(lambda r_, x: r_.astype(F32) - x[:, :W], rs, x2)
    y0 = each(lambda x1_, x: x1_[HT:] - x[:, W:], x1, x2)
    gd = each(_dot_tn, bg, pq)
    kv = each(_dot_tn, kg, vs)
    gmat = each(lambda e, t: jnp.where(eye, e[T - 1:T, :], 0.0) - t[:, :W], e_in, gd)
    dmat = each(lambda kv_, t: kv_ - t[:, W:], kv, gd)

    h = h_s[...]
    ys = []
    for c in range(C):
        ys.append(_dot(rp[c], h) + y0[c])
        h = _dot(gmat[c], h) + dmat[c]
    h_s[...] = h
    y = jnp.concatenate([t[0:T] + t[T:2 * T] + t[2 * T:3 * T] + t[3 * T:4 * T] for t in ys], axis=0)

    inv_n = 1.0 / A_HEAD_DIM
    mean = _dot(y, ebd) * inv_n
    yc = y - mean
    var = _dot(yc * yc, ebd) * inv_n
    yn = yc * lax.rsqrt(var + A_GN_EPS) * lnw_ref[...] + lnb_ref[...]
    out_ref[...] = (yn + bon) * g


def _cumsum_rows(x, ltri):
    hi, mid, lo = _split3(x)
    return (jnp.dot(ltri, hi, preferred_element_type=F32)
            + jnp.dot(ltri, mid, preferred_element_type=F32)
            + jnp.dot(ltri, lo, preferred_element_type=F32))


def _rwkv(pa, p, bsz, seq, ts):
    nt = seq // ts
    row = lambda b, i: (b * nt + i, 0)
    const = lambda b, i: (0, 0)
    vec = lambda w: pl.BlockSpec((1, w), const)
    return pl.pallas_call(
        functools.partial(_rwkv_body, ts=ts),
        grid=(bsz, nt),
        in_specs=[pl.BlockSpec((ts, A_PAD), row), vec(A_PAD), vec(A_WIDTH),
                  pl.BlockSpec((128, A_WIDTH), const), vec(A_WIDTH),
                  pl.BlockSpec((128, A_WIDTH), const),
                  pl.BlockSpec((A_WIDTH, A_WIDTH), const),
                  vec(A_WIDTH), vec(A_WIDTH), vec(A_WIDTH), vec(A_WIDTH), vec(A_WIDTH)],
        out_specs=pl.BlockSpec((ts, A_WIDTH), row),
        out_shape=jax.ShapeDtypeStruct((bsz * seq, A_WIDTH), F32),
        scratch_shapes=[pltpu.VMEM((8, A_PAD), F32), pltpu.VMEM((A_WIDTH, A_WIDTH), F32)],
        compiler_params=_cparams(("parallel", "arbitrary")),
        name="rwkv7",
    )(pa, p["mu"], p["w0"], p["dup"], p["a0"], p["iup"], p["gup"], p["k_k"], p["k_a"],
      p["r_k"], p["ln_w"], p["ln_b"])


def _dilated_body(q_ref, k_ref, v_ref, o_ref, l_ref, *, nb, gs, dil):
    Q = B_BLOCK
    HQ = B_HEADS * Q
    W = B_WIDTH
    rr = lax.broadcasted_iota(jnp.int32, (HQ, W), 0)
    cc = lax.broadcasted_iota(jnp.int32, (HQ, W), 1)
    hsel = jnp.where((rr // Q) == (cc // B_HEAD_DIM), 1.0, 0.0).astype(BF16)
    qi = rr % Q
    dist = qi + Q - cc
    band2 = (dist >= 0) & (dist <= Q)
    lane_h = lax.broadcasted_iota(jnp.int32, (Q, W), 1) // B_HEAD_DIM

    def where(n):
        if gs >= Q:
            per = gs // Q
            return n // per, pl.ds(pl.multiple_of((n % per) * Q, Q), Q)
        per = Q // gs
        return pl.ds(pl.multiple_of(n * per, per), per), slice(None)

    def load(ref, r, n):
        g, rows = where(n)
        return ref[g, r, rows, :].reshape(Q, W)

    def store(ref, r, n, val):
        g, rows = where(n)
        ref[g, r, rows, :] = val.reshape(ref[g, r, rows, :].shape)

    def each(fn, *lists):
        return [fn(*xs) for xs in zip(*lists)]

    def attend(r, ns):
        first = [n == 0 for n in ns]
        prev = [jnp.maximum(n - 1, 0) for n in ns]
        qs = [jnp.concatenate([load(q_ref, r, n)] * B_HEADS, axis=0) * hsel for n in ns]
        kw = [jnp.concatenate([load(k_ref, r, p), load(k_ref, r, n)], axis=0) for p, n in zip(prev, ns)]
        vw = [jnp.concatenate([load(v_ref, r, p), load(v_ref, r, n)], axis=0) for p, n in zip(prev, ns)]
        s = each(lambda a, b: lax.dot_general(a, b, (((1,), (1,)), ((), ())),
                                              preferred_element_type=F32), qs, kw)
        s = each(lambda t, f: jnp.where(band2 & ((cc >= Q) | jnp.logical_not(f)), t, NEG_BIG), s, first)
        m = each(lambda t: jnp.max(t, axis=-1, keepdims=True), s)
        p = each(lambda t, m_: jnp.exp(t - m_), s, m)
        l = each(lambda t: jnp.sum(t, axis=-1, keepdims=True), p)
        o = each(lambda p_, v_, l_: jnp.dot(p_.astype(BF16), v_, preferred_element_type=F32) / l_,
                 p, vw, l)
        lse = each(lambda m_, l_: jnp.broadcast_to(m_ + jnp.log(l_), (HQ, W)), m, l)
        for n, o_, lse_ in zip(ns, o, lse):
            out = o_[0:Q]
            lout = lse_[0:Q]
            for h in range(1, B_HEADS):
                out = jnp.where(lane_h == h, o_[h * Q:(h + 1) * Q], out)
                lout = jnp.where(lane_h == h, lse_[h * Q:(h + 1) * Q], lout)
            store(o_ref, r, n, out.astype(o_ref.dtype))
            store(l_ref, r, n, lout)

    unroll = min(DIL_UNROLL, nb)
    per_res = nb // unroll

    def body(t, carry):
        t0 = (t % per_res) * unroll
        attend(t // per_res, [t0 + i for i in range(unroll)])
        return carry

    lax.fori_loop(0, dil * per_res, body, 0)


def _dilated_branch(q, k, v, bsz, seq, dil, tm):
    gs = tm // dil
    nt = seq // tm
    view = lambda t: t.reshape(bsz, nt, dil, gs, B_WIDTH)
    spec = pl.BlockSpec((None, nt, dil, gs, B_WIDTH), lambda b: (b, 0, 0, 0, 0))
    o, l = pl.pallas_call(
        functools.partial(_dilated_body, nb=seq // dil // B_BLOCK, gs=gs, dil=dil),
        grid=(bsz,),
        in_specs=[spec, spec, spec],
        out_specs=[spec, spec],
        out_shape=[jax.ShapeDtypeStruct((bsz, nt, dil, gs, B_WIDTH), BF16),
                   jax.ShapeDtypeStruct((bsz, nt, dil, gs, B_WIDTH), F32)],
        compiler_params=_cparams(("parallel",)),
        name="dilated_d%d" % dil,
    )(view(q), view(k), view(v))
    return o.reshape(bsz * seq, B_WIDTH), l.reshape(bsz * seq, B_WIDTH)


def _flash_body(q_ref, k_ref, v_ref, o_ref, *, tq, nq):
    rr = lax.broadcasted_iota(jnp.int32, (tq, tq), 0)
    cc = lax.broadcasted_iota(jnp.int32, (tq, tq), 1)
    causal = cc <= rr
    reps = tq // 128
    blocks = [(i, j) for i in range(nq) for j in range(i + 1)]

    def scores(i, j):
        return lax.dot_general(q_ref[0, 0, i * tq:(i + 1) * tq, :], k_ref[0, 0, j * tq:(j + 1) * tq, :],
                               (((1,), (1,)), ((), ())), preferred_element_type=F32)

    s_next = scores(*blocks[0])
    m = l = acc = None
    for idx, (i, j) in enumerate(blocks):
        s = s_next
        if idx + 1 < len(blocks):
            s_next = scores(*blocks[idx + 1])
        if j == 0:
            m = jnp.full((tq, C_V_DIM), NEG_BIG, F32)
            l = jnp.zeros((tq, C_V_DIM), F32)
            acc = jnp.zeros((tq, C_V_DIM), F32)
        if j == i:
            s = jnp.where(causal, s, NEG_BIG)
        m_new = jnp.maximum(m, jnp.max(s, axis=-1, keepdims=True))
        alpha = jnp.exp(m - m_new)
        p = jnp.exp(s - jnp.concatenate([m_new] * reps, axis=1))
        pv = jnp.dot(p.astype(BF16), v_ref[0, 0, j * tq:(j + 1) * tq, :], preferred_element_type=F32)
        l = alpha * l + pv[:, C_V_DIM:]
        acc = alpha * acc + pv[:, :C_V_DIM]
        m = m_new
        if j == i:
            o_ref[0, i * tq:(i + 1) * tq, :] = acc / l


def _flash(q, k, v, bsz, seq, tq):
    nq = seq // tq
    hb = lambda w: pl.BlockSpec((1, 1, seq, w), lambda b, h: (b, h, 0, 0))
    return pl.pallas_call(
        functools.partial(_flash_body, tq=tq, nq=nq),
        grid=(bsz, C_HEADS),
        in_specs=[hb(C_QK_PAD), hb(C_QK_PAD), hb(2 * C_V_DIM)],
        out_specs=pl.BlockSpec((1, seq, C_V_DIM), lambda b, h: (b, 0, h)),
        out_shape=jax.ShapeDtypeStruct((bsz, seq, C_WIDTH), F32),
        compiler_params=_cparams(("parallel", "parallel")),
        name="mla_flash",
    )(q, k, v)


def _ffn_body(x_ref, ya_ref, o1_ref, o2_ref, o3_ref, l1_ref, l2_ref, l3_ref, yc_ref,
              wo_ref, g_ref, wg_ref, wu_ref, wd_ref, fg_ref, out_ref, h_s, slab_s, *, final):
    tm = x_ref.shape[0]

    def natural(dil, o_ref, l_ref):
        gs = tm // dil
        for r in range(dil):
            rows = pl.ds(r, gs, stride=dil)
            ov = o_ref[r * gs:(r + 1) * gs, :].astype(F32)
            lv = l_ref[r * gs:(r + 1) * gs, :]
            slab_s[0, rows, :] = ov[:, :128]
            slab_s[1, rows, :] = ov[:, 128:]
            slab_s[2, rows, :] = lv[:, :128]
            slab_s[3, rows, :] = lv[:, 128:]
        return (jnp.concatenate([slab_s[0], slab_s[1]], axis=1),
                jnp.concatenate([slab_s[2], slab_s[3]], axis=1))

    mix = (jnp.dot(ya_ref[...].astype(BF16), wo_ref[0:A_WIDTH, :], preferred_element_type=F32)
           + jnp.dot(yc_ref[...].astype(BF16), wo_ref[A_WIDTH + B_WIDTH:, :],
                     preferred_element_type=F32))
    o1, l1 = o1_ref[...].astype(F32), l1_ref[...]
    o2, l2 = natural(B_PATTERNS[1][1], o2_ref, l2_ref)
    o3, l3 = natural(B_PATTERNS[2][1], o3_ref, l3_ref)
    mx = jnp.maximum(jnp.maximum(l1, l2), l3)
    e1, e2, e3 = jnp.exp(l1 - mx), jnp.exp(l2 - mx), jnp.exp(l3 - mx)
    yb = (e1 * o1 + e2 * o2 + e3 * o3) / (e1 + e2 + e3)
    mix = mix + jnp.dot(yb.astype(BF16), wo_ref[A_WIDTH:A_WIDTH + B_WIDTH, :],
                        preferred_element_type=F32)
    xn = x_ref[...] + mix
    out_ref[...] = xn
    h_s[...] = _rms(xn, g_ref[...]).astype(BF16)

    for j in range(D_FF // FF_CHUNK):
        cols = slice(j * FF_CHUNK, (j + 1) * FF_CHUNK)
        h = h_s[...]
        gt = jnp.dot(h, wg_ref[:, cols], preferred_element_type=F32)
        up = jnp.dot(h, wu_ref[:, cols], preferred_element_type=F32)
        act = gt * _sigmoid(gt) * up
        out_ref[...] += jnp.dot(act.astype(BF16), wd_ref[cols, :], preferred_element_type=F32)
    if final:
        out_ref[...] = _rms(out_ref[...], fg_ref[...])


def _ffn(x2, ya, ob, lb, yc, wo, g, wg, wu, wd, fg, tm, final):
    n = x2.shape[0]
    row = lambda i: (i, 0)
    rb = lambda w: pl.BlockSpec((tm, w), row)
    res = lambda a: pl.BlockSpec(a.shape, lambda i: (0,) * a.ndim, pipeline_mode=pl.Buffered(1))
    return pl.pallas_call(
        functools.partial(_ffn_body, final=final),
        grid=(n // tm,),
        in_specs=[rb(D_MODEL), rb(A_WIDTH), rb(B_WIDTH), rb(B_WIDTH), rb(B_WIDTH),
                  rb(B_WIDTH), rb(B_WIDTH), rb(B_WIDTH), rb(C_WIDTH),
                  res(wo), res(g), res(wg), res(wu), res(wd), res(fg)],
        out_specs=pl.BlockSpec((tm, D_MODEL), row),
        out_shape=jax.ShapeDtypeStruct((n, D_MODEL), F32),
        scratch_shapes=[pltpu.VMEM((tm, D_MODEL), BF16), pltpu.VMEM((4, tm, 128), F32)],
        compiler_params=_cparams(("parallel",)),
        name="out_proj_ffn",
    )(x2, ya, ob[0], ob[1], ob[2], lb[0], lb[1], lb[2], yc, wo, g, wg, wu, wd, fg)


def _pad_cols(w, width):
    return jnp.pad(w, ((0, 0), (0, width - w.shape[1])))


def _rope_layout_cols(w):
    z = jnp.zeros(w.shape[:-1] + (32,), w.dtype)
    return jnp.concatenate([w[..., :32], z, w[..., 32:], z], axis=-1)


def _layer_params(l, w_in, a_mu, a_w0, a_decay_up, a_a0, a_iclr_up, a_gate_up, a_k_k, a_k_a,
                  a_r_k, a_ln_w, a_ln_b, c_w_uq, c_w_ukv):
    wi = w_in[l]
    oa = A_PROJ
    ob = oa + 3 * B_WIDTH
    w_all = jnp.concatenate([
        _pad_cols(wi[:, :oa], A_PAD),
        wi[:, oa:ob],
        wi[:, ob:ob + C_Q_LORA + C_KV_LORA],
        _rope_layout_cols(wi[:, ob + C_Q_LORA + C_KV_LORA:]),
    ], axis=1).astype(BF16)
    z64 = jnp.zeros((64, A_WIDTH), F32)
    rw = lambda t: t.reshape(1, -1)
    wq = c_w_uq[l].reshape(C_Q_LORA, C_HEADS, C_NOPE_DIM + C_ROPE_DIM)
    wq = jnp.concatenate([wq[..., :C_NOPE_DIM], _rope_layout_cols(wq[..., C_NOPE_DIM:])], axis=-1)
    wkv = c_w_ukv[l].reshape(C_KV_LORA, C_HEADS, C_NOPE_DIM + C_V_DIM)
    wkv = jnp.concatenate([wkv[..., :C_NOPE_DIM].reshape(C_KV_LORA, -1),
                           wkv[..., C_NOPE_DIM:].reshape(C_KV_LORA, -1)], axis=1)
    return dict(
        w_all=w_all,
        mu=_pad_cols(rw(a_mu[l]), A_PAD), w0=rw(a_w0[l]),
        dup=jnp.concatenate([a_decay_up[l], z64], axis=0),
        a0=rw(a_a0[l]),
        iup=jnp.concatenate([z64, a_iclr_up[l]], axis=0),
        gup=jnp.pad(a_gate_up[l], ((0, A_WIDTH - A_GATE_LORA), (0, 0))),
        k_k=rw(a_k_k[l]), k_a=rw(a_k_a[l]), r_k=rw(a_r_k[l]),
        ln_w=rw(a_ln_w[l]), ln_b=rw(a_ln_b[l]),
        wq=wq.reshape(C_Q_LORA, C_HEADS * C_QK_PAD).astype(BF16),
        wkv=wkv.astype(BF16),
    )


def _rope_tables(positions):
    n = positions.size
    pos = positions.reshape(n, 1).astype(F32)

    def tables(dim):
        inv_freq = 1.0 / (ROPE_THETA ** (jnp.arange(0, dim, 2, dtype=F32) / dim))
        ang = pos * inv_freq
        return jnp.cos(ang), jnp.sin(ang)

    cb, sb = tables(B_ROT_DIM)
    one = jnp.ones((n, B_HEAD_DIM - B_ROT_DIM), F32)
    cb = jnp.tile(jnp.concatenate([cb, cb, one], axis=1), (1, 128 // B_HEAD_DIM))
    sb = jnp.tile(jnp.concatenate([-sb, sb, 0.0 * one], axis=1), (1, 128 // B_HEAD_DIM))
    cc, sc = tables(C_ROPE_DIM)
    z = jnp.zeros_like(cc)
    cc = jnp.concatenate([cc, z, cc, z], axis=1)
    sc = jnp.concatenate([-sc, z, sc, z], axis=1)
    return cb, sb, cc, sc


def kernel(x, positions, attn_norm_g, w_in, a_mu, a_w0, a_decay_up, a_a0, a_iclr_up, a_gate_up, a_k_k, a_k_a, a_r_k, a_ln_w, a_ln_b, c_q_norm_g, c_kv_norm_g, c_w_uq, c_w_ukv, w_out, ffn_norm_g, ffn_w_gate, ffn_w_up, ffn_w_down, final_norm_g):
    bsz, seq, _ = x.shape
    n = bsz * seq
    depth = w_in.shape[0]
    assert seq % max(w for w, _ in B_PATTERNS) == 0
    cb, sb, cc, sc = _rope_tables(positions)
    x2 = x.reshape(n, D_MODEL)
    tm = min(512, seq)
    for l in range(depth):
        p = _layer_params(l, w_in, a_mu, a_w0, a_decay_up, a_a0, a_iclr_up, a_gate_up, a_k_k,
                          a_k_a, a_r_k, a_ln_w, a_ln_b, c_w_uq, c_w_ukv)
        pa, *qkv = _in_proj(x2, attn_norm_g[l].reshape(1, -1), p["w_all"], cb, sb,
                            c_q_norm_g[l].reshape(1, -1), c_kv_norm_g[l].reshape(1, -1),
                            p["wq"], p["wkv"], cc, sc, bsz, seq, tm)
        qc, kc, vc = qkv[9:]
        ya = _rwkv(pa, p, bsz, seq, min(RWKV_TILE, seq))
        ob, lb = zip(*[_dilated_branch(*qkv[3 * i:3 * i + 3], bsz, seq, d, tm)
                       for i, (_, d) in enumerate(B_PATTERNS)])
        yc = _flash(qc, kc, vc, bsz, seq, min(512, seq)).reshape(n, C_WIDTH)
        x2 = _ffn(x2, ya, ob, lb, yc, w_out[l].astype(BF16), ffn_norm_g[l].reshape(1, -1),
                  ffn_w_gate[l].astype(BF16), ffn_w_up[l].astype(BF16), ffn_w_down[l].astype(BF16),
                  final_norm_g.reshape(1, -1), tm, l == depth - 1)
    return x2.reshape(bsz, seq, D_MODEL)
```

```python
import functools

import jax
import jax.numpy as jnp
from jax import lax
from jax.experimental import pallas as pl
from jax.experimental.pallas import tpu as pltpu

F32 = jnp.float32
BF16 = jnp.bfloat16

D_MODEL = 1024
NORM_EPS = 1e-6
ROPE_THETA = 500000.0

A_HEADS = 4
A_HEAD_DIM = 64
A_WIDTH = 256
A_DECAY_LORA = 64
A_ICLR_LORA = 64
A_GATE_LORA = 160
A_GN_EPS = 64e-5
A_PROJ = 3 * A_WIDTH + A_DECAY_LORA + A_ICLR_LORA + A_GATE_LORA
A_PAD = 1152
A_CHUNK = 64
RWKV_TILE = 512

B_HEADS = 4
B_HEAD_DIM = 64
B_WIDTH = 256
B_ROT_DIM = 16
B_PATTERNS = ((128, 1), (512, 4), (2048, 16))
B_BLOCK = 128
DIL_UNROLL = 2

C_HEADS = 4
C_NOPE_DIM = 128
C_ROPE_DIM = 64
C_V_DIM = 128
C_Q_LORA = 256
C_KV_LORA = 128
C_WIDTH = 512
C_QK_PAD = 256
C_PAD = 512

MIX_WIDTH = 1024
P_PAD = A_PAD + 3 * B_WIDTH + C_PAD
D_FF = 2816
FF_CHUNK = 256

ROW_TILE = 512
FLASH_TILE = 512
LANES = 128
VMEM_LIMIT = 56 * 1024 * 1024
NEG_BIG = -1e30


def _cparams(sem):
    return pltpu.CompilerParams(dimension_semantics=sem, vmem_limit_bytes=VMEM_LIMIT)


def _rms(x, g):
    return x * lax.rsqrt(jnp.mean(x * x, axis=-1, keepdims=True) + NORM_EPS) * g


def _dot(a, b):
    return jnp.dot(a.astype(BF16), b.astype(BF16), preferred_element_type=F32)


def _dot_nt(a, b):
    return lax.dot_general(a.astype(BF16), b.astype(BF16), (((1,), (1,)), ((), ())),
                           preferred_element_type=F32)


def _dot_tn(a, b):
    return lax.dot_general(a.astype(BF16), b.astype(BF16), (((0,), (0,)), ((), ())),
                           preferred_element_type=F32)


def _split3(x):
    hi = x.astype(BF16)
    r1 = x - hi.astype(F32)
    mid = r1.astype(BF16)
    lo = (r1 - mid.astype(F32)).astype(BF16)
    return hi, mid, lo


def _dot_hi(a, b):
    ah = a.astype(BF16)
    al = (a - ah.astype(F32)).astype(BF16)
    bh = b.astype(BF16)
    bl = (b - bh.astype(F32)).astype(BF16)
    return (jnp.dot(ah, bh, preferred_element_type=F32)
            + jnp.dot(al, bh, preferred_element_type=F32)
            + jnp.dot(ah, bl, preferred_element_type=F32))


def _sigmoid(x):
    return 1.0 / (1.0 + jnp.exp(-x))


def _in_proj_body(x_ref, g_ref, w_ref, cb_ref, sb_ref, qg_ref, kvg_ref, wq_ref, wkv_ref, cc_ref, sc_ref,
                  pa_ref, q1_ref, k1_ref, v1_ref, q4_ref, k4_ref, v4_ref, q16_ref, k16_ref, v16_ref,
                  qc_ref, kc_ref, vc_ref, slab_s):
    tm = x_ref.shape[0]
    h = _rms(x_ref[...], g_ref[...])
    y = jnp.dot(h.astype(BF16), w_ref[...], preferred_element_type=F32)
    pa_ref[...] = y[:, :A_PAD]
    cb = jnp.concatenate([cb_ref[...]] * 2, axis=1)
    sb = jnp.concatenate([sb_ref[...]] * 2, axis=1)
    lane = lax.broadcasted_iota(jnp.int32, (1, B_WIDTH), 1)
    first = (lane % B_HEAD_DIM) < (B_ROT_DIM // 2)

    def rope(t):
        partner = jnp.where(first, pltpu.roll(t, B_WIDTH - B_ROT_DIM // 2, 1),
                            pltpu.roll(t, B_ROT_DIM // 2, 1))
        return t * cb + partner * sb

    o = A_PAD
    qkv = (rope(y[:, o:o + B_WIDTH]) * (B_HEAD_DIM ** -0.5),
           rope(y[:, o + B_WIDTH:o + 2 * B_WIDTH]),
           y[:, o + 2 * B_WIDTH:o + 3 * B_WIDTH])
    _mla_up(y[:, o + 3 * B_WIDTH:], qg_ref, kvg_ref, wq_ref, wkv_ref, cc_ref, sc_ref,
            qc_ref, kc_ref, vc_ref)
    for i, (t, o_ref) in enumerate(zip(qkv, (q1_ref, k1_ref, v1_ref))):
        o_ref[...] = t.astype(BF16)
        slab_s[2 * i] = t[:, :LANES]
        slab_s[2 * i + 1] = t[:, LANES:]
    for dil, outs in ((4, (q4_ref, k4_ref, v4_ref)), (16, (q16_ref, k16_ref, v16_ref))):
        gs = tm // dil
        for i, o_ref in enumerate(outs):
            for r in range(dil):
                rows = pl.ds(r, gs, stride=dil)
                o_ref[r * gs:(r + 1) * gs, :] = jnp.concatenate(
                    [slab_s[2 * i, rows, :], slab_s[2 * i + 1, rows, :]], axis=1).astype(BF16)


def _mla_up(pc, qg_ref, kvg_ref, wq_ref, wkv_ref, cc_ref, sc_ref, q_ref, k_ref, v_ref):
    cq = _rms(pc[:, :C_Q_LORA], qg_ref[...])
    ckv = _rms(pc[:, C_Q_LORA:C_Q_LORA + C_KV_LORA], kvg_ref[...])
    kr = pc[:, C_Q_LORA + C_KV_LORA:]
    scale = (C_NOPE_DIM + C_ROPE_DIM) ** -0.5
    q = jnp.dot(cq.astype(BF16), wq_ref[...], preferred_element_type=F32) * scale
    kv = jnp.dot(ckv.astype(BF16), wkv_ref[...], preferred_element_type=F32)
    cc = cc_ref[...]
    sc = sc_ref[...]

    def rope(t):
        return t * cc + pltpu.roll(t, 64, 1) * sc

    krr = rope(kr).astype(BF16)
    for h in range(C_HEADS):
        o = h * C_QK_PAD
        q_ref[0, h, :, 0:C_NOPE_DIM] = q[:, o:o + C_NOPE_DIM].astype(BF16)
        q_ref[0, h, :, C_NOPE_DIM:] = rope(q[:, o + C_NOPE_DIM:o + C_QK_PAD]).astype(BF16)
        k_ref[0, h, :, 0:C_NOPE_DIM] = kv[:, h * C_NOPE_DIM:(h + 1) * C_NOPE_DIM].astype(BF16)
        k_ref[0, h, :, C_NOPE_DIM:] = krr
        vo = C_HEADS * C_NOPE_DIM + h * C_V_DIM
        v_ref[0, h, :, 0:C_V_DIM] = kv[:, vo:vo + C_V_DIM].astype(BF16)
        v_ref[0, h, :, C_V_DIM:] = jnp.ones((kv.shape[0], C_V_DIM), BF16)


def _in_proj(x2, g, w, cb, sb, qg, kvg, wq, wkv, cc, sc, bsz, seq, tm):
    n = x2.shape[0]
    nt = seq // tm
    row = lambda i: (i, 0)
    rb = lambda wd: pl.BlockSpec((tm, wd), row)
    res = lambda a: pl.BlockSpec(a.shape, lambda i: (0,) * a.ndim, pipeline_mode=pl.Buffered(1))
    hm = lambda wd: pl.BlockSpec((1, C_HEADS, tm, wd), lambda i: (i // nt, 0, i % nt, 0))
    bsd = jax.ShapeDtypeStruct((n, B_WIDTH), BF16)
    hsd = lambda wd: jax.ShapeDtypeStruct((bsz, C_HEADS, seq, wd), BF16)
    return pl.pallas_call(
        _in_proj_body,
        grid=(n // tm,),
        in_specs=[rb(D_MODEL), res(g), res(w), rb(LANES), rb(LANES),
                  res(qg), res(kvg), res(wq), res(wkv), rb(LANES), rb(LANES)],
        out_specs=[rb(A_PAD)] + [rb(B_WIDTH)] * 9 + [hm(C_QK_PAD), hm(C_QK_PAD), hm(2 * C_V_DIM)],
        out_shape=[jax.ShapeDtypeStruct((n, A_PAD), F32)] + [bsd] * 9
                  + [hsd(C_QK_PAD), hsd(C_QK_PAD), hsd(2 * C_V_DIM)],
        scratch_shapes=[pltpu.VMEM((6, tm, LANES), F32)],
        compiler_params=_cparams(("parallel",)),
        name="in_proj",
    )(x2, g, w, cb, sb, qg, kvg, wq, wkv, cc, sc)


def _rwkv_body(pa_ref, mu_ref, w0_ref, dup_ref, a0_ref, iup_ref, gup_ref, kk_ref, ka_ref,
               rk_ref, lnw_ref, lnb_ref, out_ref, prev_s, h_s, *, ts):
    T = A_CHUNK
    HT = A_HEADS * T
    W = A_WIDTH
    C = ts // T

    @pl.when(pl.program_id(1) == 0)
    def _():
        prev_s[...] = jnp.zeros_like(prev_s)
        h_s[...] = jnp.zeros_like(h_s)

    lane_w = lax.broadcasted_iota(jnp.int32, (W, W), 1)
    row_w = lax.broadcasted_iota(jnp.int32, (W, W), 0)
    ebd = jnp.where(lane_w // A_HEAD_DIM == row_w // A_HEAD_DIM, 1.0, 0.0).astype(F32)

    pa = pa_ref[...]
    rid = lax.broadcasted_iota(jnp.int32, (ts, 1), 0)
    shifted = jnp.where(rid == 0, prev_s[0:1, :], pltpu.roll(pa, 1, 0))
    prev_s[0:1, :] = pa[ts - 1:ts, :]
    pf = pa + (shifted - pa) * mu_ref[...]
    r = pf[:, 0:W]
    k = pf[:, W:2 * W]
    v = pf[:, 2 * W:3 * W]
    xwa = pf[:, 3 * W:3 * W + LANES]
    xg = pf[:, 3 * W + LANES:]
    dl = _dot_hi(jnp.tanh(xwa), dup_ref[...])
    z = -(w0_ref[...] + dl)
    softplus = jnp.maximum(z, 0.0) + jnp.log(1.0 + jnp.exp(-jnp.abs(z)))
    w_log = -softplus - 0.5
    lw = -jnp.exp(w_log)
    a = _sigmoid(a0_ref[...] + _dot(xwa, iup_ref[...]))
    g = _dot(_sigmoid(xg), gup_ref[...])
    kkv = k * kk_ref[...]
    ss = _dot(kkv * kkv, ebd)
    kn = kkv / jnp.maximum(jnp.sqrt(ss), 1e-12)
    k2 = k * (1.0 + (a - 1.0) * ka_ref[...])
    bon = _dot(r * k2 * rk_ref[...], ebd) * v
    b = kn * a

    rr = lax.broadcasted_iota(jnp.int32, (HT, HT), 0)
    cc = lax.broadcasted_iota(jnp.int32, (HT, HT), 1)
    strict = rr > cc
    lower = rr >= cc
    eye = rr == cc
    hmask = (rr // T) == (cc // A_HEAD_DIM)
    tr = lax.broadcasted_iota(jnp.int32, (T, T), 0)
    tc = lax.broadcasted_iota(jnp.int32, (T, T), 1)
    ltri = jnp.where(tr >= tc, 1.0, 0.0).astype(BF16)

    def stack(x):
        return jnp.where(hmask, jnp.concatenate([x] * A_HEADS, axis=0), 0.0).astype(BF16)

    def chunks(x):
        return [x[c * T:(c + 1) * T, :] for c in range(C)]

    def each(fn, *lists):
        return [fn(*xs) for xs in zip(*lists)]

    rc, kc, vc, knc, bc, lwc = (chunks(t) for t in (r, k2, v, kn, b, lw))
    cum = each(lambda x: _cumsum_rows(x, ltri), lwc)
    e_in = each(jnp.exp, cum)
    rs = each(lambda x, e: stack(x * e), rc, e_in)
    ks_ = each(lambda x, cu, l: stack(x * jnp.exp(cu - l)), knc, cum, lwc)
    e_out = each(lambda cu: jnp.exp(-cu), cum)
    bh = each(lambda x, e: stack(x * e), bc, e_out)
    kh = each(lambda x, e: stack(x * e), kc, e_out)
    e_end = each(lambda cu: jnp.exp(cu[T - 1:T, :] - cu), cum)
    bg = each(lambda x, e: stack(x * e), bc, e_end)
    kg = each(lambda x, e: stack(x * e), kc, e_end)
    vs = each(stack, vc)

    a4 = each(lambda k_, r_, b_, kh_: _dot_nt(jnp.concatenate([k_, r_], axis=0),
                                              jnp.concatenate([b_, kh_], axis=0)), ks_, rs, bh, kh)
    a_ab = each(lambda t: jnp.where(strict, t[:HT, :HT], 0.0), a4)
    a_akrk = each(lambda t: jnp.concatenate([jnp.where(strict, t[:HT, HT:], 0.0),
                                             jnp.where(lower, t[HT:, HT:], 0.0)], axis=0), a4)
    a_rb = each(lambda t: jnp.where(lower, t[HT:, :HT], 0.0), a4)

    m = each(lambda t: jnp.where(eye, 1.0, 0.0) - jnp.where((rr // 2 == cc // 2), t, 0.0), a_ab)
    s = 2
    while s < T:
        blk = (rr // (2 * s) == cc // (2 * s)) & ((rr // s) % 2 == 1) & ((cc // s) % 2 == 0)
        if s % 8:
            cm = each(lambda t, m_: _dot(jnp.where(blk, t, 0.0), m_), a_ab, m)
            m = each(lambda m_, t: m_ - _dot(m_, t), m, cm)
        else:
            nblk = HT // s
            odd = lambda t: jnp.concatenate([t[g * s:(g + 1) * s] for g in range(1, nblk, 2)], axis=0)
            zero = jnp.zeros((s, HT), F32)

            def spread(t_odd, even_of=None):
                parts = []
                for g in range(nblk):
                    if g % 2:
                        parts.append(t_odd[(g // 2) * s:(g // 2 + 1) * s])
                    else:
                        parts.append(zero if even_of is None else even_of[g * s:(g + 1) * s])
                return jnp.concatenate(parts, axis=0)

            blk_odd = odd(jnp.where(blk, 1.0, 0.0)) > 0.5
            cm = each(lambda t, m_: spread(_dot(jnp.where(blk_odd, odd(t), 0.0), m_)), a_ab, m)
            m = each(lambda m_, t: spread(odd(m_) - _dot(odd(m_), t), m_), m, cm)
        s *= 2

    x1 = each(_dot, a_akrk, vs)
    pq = each(lambda m_, k_, x: _dot(m_, jnp.concatenate([k_, x[:HT].astype(BF16)], axis=1)),
              m, ks_, x1)
    x2 = each(_dot, a_rb, pq)
    rp = each(lambda r_, x: r_.astype(F32) - x[:, :W], rs, x2)
    y0 = each(lambda x1_, x: x1_[HT:] - x[:, W:], x1, x2)
    gd = each(_dot_tn, bg, pq)
    kv = each(_dot_tn, kg, vs)
    gmat = each(lambda e, t: jnp.where(eye, e[T - 1:T, :], 0.0) - t[:, :W], e_in, gd)
    dmat = each(lambda kv_, t: kv_ - t[:, W:], kv, gd)

    h = h_s[...]
    ys = []
    for c in range(C):
        ys.append(_dot(rp[c], h) + y0[c])
        h = _dot(gmat[c], h) + dmat[c]
    h_s[...] = h
    y = jnp.concatenate([t[0:T] + t[T:2 * T] + t[2 * T:3 * T] + t[3 * T:4 * T] for t in ys], axis=0)

    inv_n = 1.0 / A_HEAD_DIM
    mean = _dot(y, ebd) * inv_n
    yc = y - mean
    var = _dot(yc * yc, ebd) * inv_n
    yn = yc * lax.rsqrt(var + A_GN_EPS) * lnw_ref[...] + lnb_ref[...]
    out_ref[...] = (yn + bon) * g


def _cumsum_rows(x, ltri):
    hi, mid, lo = _split3(x)
    return (jnp.dot(ltri, hi, preferred_element_type=F32)
            + jnp.dot(ltri, mid, preferred_element_type=F32)
            + jnp.dot(ltri, lo, preferred_element_type=F32))


def _rwkv(pa, p, bsz, seq, ts):
    nt = seq // ts
    row = lambda b, i: (b * nt + i, 0)
    const = lambda b, i: (0, 0)
    vec = lambda w: pl.BlockSpec((1, w), const)
    return pl.pallas_call(
        functools.partial(_rwkv_body, ts=ts),
        grid=(bsz, nt),
        in_specs=[pl.BlockSpec((ts, A_PAD), row), vec(A_PAD), vec(A_WIDTH),
                  pl.BlockSpec((A_DECAY_LORA + A_ICLR_LORA, A_WIDTH), const), vec(A_WIDTH),
                  pl.BlockSpec((A_DECAY_LORA + A_ICLR_LORA, A_WIDTH), const),
                  pl.BlockSpec((A_WIDTH, A_WIDTH), const),
                  vec(A_WIDTH), vec(A_WIDTH), vec(A_WIDTH), vec(A_WIDTH), vec(A_WIDTH)],
        out_specs=pl.BlockSpec((ts, A_WIDTH), row),
        out_shape=jax.ShapeDtypeStruct((bsz * seq, A_WIDTH), F32),
        scratch_shapes=[pltpu.VMEM((8, A_PAD), F32), pltpu.VMEM((A_WIDTH, A_WIDTH), F32)],
        compiler_params=_cparams(("parallel", "arbitrary")),
        name="rwkv7",
    )(pa, p["mu"], p["w0"], p["dup"], p["a0"], p["iup"], p["gup"], p["k_k"], p["k_a"],
      p["r_k"], p["ln_w"], p["ln_b"])


def _dilated_body(q_ref, k_ref, v_ref, o_ref, l_ref, *, nb, gs, dil):
    Q = B_BLOCK
    HQ = B_HEADS * Q
    W = B_WIDTH
    rr = lax.broadcasted_iota(jnp.int32, (HQ, W), 0)
    cc = lax.broadcasted_iota(jnp.int32, (HQ, W), 1)
    hsel = jnp.where((rr // Q) == (cc // B_HEAD_DIM), 1.0, 0.0).astype(BF16)
    qi = rr % Q
    dist = qi + Q - cc
    band2 = (dist >= 0) & (dist <= Q)
    lane_h = lax.broadcasted_iota(jnp.int32, (Q, W), 1) // B_HEAD_DIM

    def where(n):
        if gs >= Q:
            per = gs // Q
            return n // per, pl.ds(pl.multiple_of((n % per) * Q, Q), Q)
        per = Q // gs
        return pl.ds(pl.multiple_of(n * per, per), per), slice(None)

    def load(ref, r, n):
        g, rows = where(n)
        return ref[g, r, rows, :].reshape(Q, W)

    def store(ref, r, n, val):
        g, rows = where(n)
        ref[g, r, rows, :] = val.reshape(ref[g, r, rows, :].shape)

    def each(fn, *lists):
        return [fn(*xs) for xs in zip(*lists)]

    def attend(r, ns):
        first = [n == 0 for n in ns]
        prev = [jnp.maximum(n - 1, 0) for n in ns]
        qs = [jnp.concatenate([load(q_ref, r, n)] * B_HEADS, axis=0) * hsel for n in ns]
        kw = [jnp.concatenate([load(k_ref, r, p), load(k_ref, r, n)], axis=0) for p, n in zip(prev, ns)]
        vw = [jnp.concatenate([load(v_ref, r, p), load(v_ref, r, n)], axis=0) for p, n in zip(prev, ns)]
        s = each(lambda a, b: lax.dot_general(a, b, (((1,), (1,)), ((), ())),
                                              preferred_element_type=F32), qs, kw)
        s = each(lambda t, f: jnp.where(band2 & ((cc >= Q) | jnp.logical_not(f)), t, NEG_BIG), s, first)
        m = each(lambda t: jnp.max(t, axis=-1, keepdims=True), s)
        p = each(lambda t, m_: jnp.exp(t - m_), s, m)
        l = each(lambda t: jnp.sum(t, axis=-1, keepdims=True), p)
        o = each(lambda p_, v_, l_: jnp.dot(p_.astype(BF16), v_, preferred_element_type=F32) / l_,
                 p, vw, l)
        lse = each(lambda m_, l_: jnp.broadcast_to(m_ + jnp.log(l_), (HQ, W)), m, l)
        for n, o_, lse_ in zip(ns, o, lse):
            out = o_[0:Q]
            lout = lse_[0:Q]
            for h in range(1, B_HEADS):
                out = jnp.where(lane_h == h, o_[h * Q:(h + 1) * Q], out)
                lout = jnp.where(lane_h == h, lse_[h * Q:(h + 1) * Q], lout)
            store(o_ref, r, n, out.astype(o_ref.dtype))
            store(l_ref, r, n, lout)

    unroll = min(DIL_UNROLL, nb)
    per_res = nb // unroll

    def body(t, carry):
        t0 = (t % per_res) * unroll
        attend(t // per_res, [t0 + i for i in range(unroll)])
        return carry

    lax.fori_loop(0, dil * per_res, body, 0)


def _dilated_branch(q, k, v, bsz, seq, dil, tm):
    gs = tm // dil
    nt = seq // tm
    view = lambda t: t.reshape(bsz, nt, dil, gs, B_WIDTH)
    spec = pl.BlockSpec((None, nt, dil, gs, B_WIDTH), lambda b: (b, 0, 0, 0, 0))
    o, l = pl.pallas_call(
        functools.partial(_dilated_body, nb=seq // dil // B_BLOCK, gs=gs, dil=dil),
        grid=(bsz,),
        in_specs=[spec, spec, spec],
        out_specs=[spec, spec],
        out_shape=[jax.ShapeDtypeStruct((bsz, nt, dil, gs, B_WIDTH), BF16),
                   jax.ShapeDtypeStruct((bsz, nt, dil, gs, B_WIDTH), F32)],
        compiler_params=_cparams(("parallel",)),
        name="dilated_d%d" % dil,
    )(view(q), view(k), view(v))
    return o.reshape(bsz * seq, B_WIDTH), l.reshape(bsz * seq, B_WIDTH)


def _flash_body(q_ref, k_ref, v_ref, o_ref, *, tq, nq):
    hq = tq // 2
    causal_top = (lax.broadcasted_iota(jnp.int32, (hq, hq), 1)
                  <= lax.broadcasted_iota(jnp.int32, (hq, hq), 0))
    causal_bot = (lax.broadcasted_iota(jnp.int32, (hq, tq), 1)
                  <= lax.broadcasted_iota(jnp.int32, (hq, tq), 0) + hq)
    blocks = [(i, j) for i in range(nq) for j in range(i + 1)]

    def qk(rows, keys):
        return lax.dot_general(q_ref[0, 0, rows, :], k_ref[0, 0, keys, :],
                               (((1,), (1,)), ((), ())), preferred_element_type=F32)

    def scores(i, j):
        if j < i:
            return (qk(slice(i * tq, (i + 1) * tq), slice(j * tq, (j + 1) * tq)),)
        return (qk(slice(i * tq, i * tq + hq), slice(j * tq, j * tq + hq)),
                qk(slice(i * tq + hq, (i + 1) * tq), slice(j * tq, (j + 1) * tq)))

    def update(m, l, acc, s, vb):
        m_new = jnp.maximum(m, jnp.max(s, axis=-1, keepdims=True))
        alpha = jnp.exp(m - m_new)
        p = jnp.exp(s - jnp.concatenate([m_new] * (s.shape[1] // LANES), axis=1))
        pv = jnp.dot(p.astype(BF16), vb, preferred_element_type=F32)
        return m_new, alpha * l + pv[:, C_V_DIM:], alpha * acc + pv[:, :C_V_DIM]

    s_next = scores(*blocks[0])
    m = l = acc = None
    for idx, (i, j) in enumerate(blocks):
        s = s_next
        if idx + 1 < len(blocks):
            s_next = scores(*blocks[idx + 1])
        if j == 0:
            m = jnp.full((tq, C_V_DIM), NEG_BIG, F32)
            l = jnp.zeros((tq, C_V_DIM), F32)
            acc = jnp.zeros((tq, C_V_DIM), F32)
        if j < i:
            m, l, acc = update(m, l, acc, s[0], v_ref[0, 0, j * tq:(j + 1) * tq, :])
        else:
            top = update(m[:hq], l[:hq], acc[:hq], jnp.where(causal_top, s[0], NEG_BIG),
                         v_ref[0, 0, j * tq:j * tq + hq, :])
            bot = update(m[hq:], l[hq:], acc[hq:], jnp.where(causal_bot, s[1], NEG_BIG),
                         v_ref[0, 0, j * tq:(j + 1) * tq, :])
            o_ref[0, i * tq:i * tq + hq, :] = top[2] / top[1]
            o_ref[0, i * tq + hq:(i + 1) * tq, :] = bot[2] / bot[1]


def _flash(q, k, v, bsz, seq, tq):
    nq = seq // tq
    hb = lambda w: pl.BlockSpec((1, 1, seq, w), lambda b, h: (b, h, 0, 0))
    return pl.pallas_call(
        functools.partial(_flash_body, tq=tq, nq=nq),
        grid=(bsz, C_HEADS),
        in_specs=[hb(C_QK_PAD), hb(C_QK_PAD), hb(2 * C_V_DIM)],
        out_specs=pl.BlockSpec((1, seq, C_V_DIM), lambda b, h: (b, 0, h)),
        out_shape=jax.ShapeDtypeStruct((bsz, seq, C_WIDTH), F32),
        compiler_params=_cparams(("parallel", "parallel")),
        name="mla_flash",
    )(q, k, v)


def _ffn_body(x_ref, ya_ref, o1_ref, o2_ref, o3_ref, l1_ref, l2_ref, l3_ref, yc_ref,
              wo_ref, g_ref, wg_ref, wu_ref, wd_ref, fg_ref, out_ref, h_s, slab_s, *, final):
    tm = x_ref.shape[0]

    def natural(dil, o_ref, l_ref):
        gs = tm // dil
        for r in range(dil):
            rows = pl.ds(r, gs, stride=dil)
            ov = o_ref[r * gs:(r + 1) * gs, :].astype(F32)
            lv = l_ref[r * gs:(r + 1) * gs, :]
            slab_s[0, rows, :] = ov[:, :LANES]
            slab_s[1, rows, :] = ov[:, LANES:]
            slab_s[2, rows, :] = lv[:, :LANES]
            slab_s[3, rows, :] = lv[:, LANES:]
        return (jnp.concatenate([slab_s[0], slab_s[1]], axis=1),
                jnp.concatenate([slab_s[2], slab_s[3]], axis=1))

    mix = (jnp.dot(ya_ref[...].astype(BF16), wo_ref[0:A_WIDTH, :], preferred_element_type=F32)
           + jnp.dot(yc_ref[...].astype(BF16), wo_ref[A_WIDTH + B_WIDTH:, :],
                     preferred_element_type=F32))
    o1, l1 = o1_ref[...].astype(F32), l1_ref[...]
    o2, l2 = natural(B_PATTERNS[1][1], o2_ref, l2_ref)
    o3, l3 = natural(B_PATTERNS[2][1], o3_ref, l3_ref)
    mx = jnp.maximum(jnp.maximum(l1, l2), l3)
    e1, e2, e3 = jnp.exp(l1 - mx), jnp.exp(l2 - mx), jnp.exp(l3 - mx)
    yb = (e1 * o1 + e2 * o2 + e3 * o3) / (e1 + e2 + e3)
    mix = mix + jnp.dot(yb.astype(BF16), wo_ref[A_WIDTH:A_WIDTH + B_WIDTH, :],
                        preferred_element_type=F32)
    xn = x_ref[...] + mix
    out_ref[...] = xn
    h_s[...] = _rms(xn, g_ref[...]).astype(BF16)

    for j in range(D_FF // FF_CHUNK):
        cols = slice(j * FF_CHUNK, (j + 1) * FF_CHUNK)
        h = h_s[...]
        gt = jnp.dot(h, wg_ref[:, cols], preferred_element_type=F32)
        up = jnp.dot(h, wu_ref[:, cols], preferred_element_type=F32)
        act = gt * _sigmoid(gt) * up
        out_ref[...] += jnp.dot(act.astype(BF16), wd_ref[cols, :], preferred_element_type=F32)
    if final:
        out_ref[...] = _rms(out_ref[...], fg_ref[...])


def _ffn(x2, ya, ob, lb, yc, wo, g, wg, wu, wd, fg, tm, final):
    n = x2.shape[0]
    row = lambda i: (i, 0)
    rb = lambda w: pl.BlockSpec((tm, w), row)
    res = lambda a: pl.BlockSpec(a.shape, lambda i: (0,) * a.ndim, pipeline_mode=pl.Buffered(1))
    return pl.pallas_call(
        functools.partial(_ffn_body, final=final),
        grid=(n // tm,),
        in_specs=[rb(D_MODEL), rb(A_WIDTH), rb(B_WIDTH), rb(B_WIDTH), rb(B_WIDTH),
                  rb(B_WIDTH), rb(B_WIDTH), rb(B_WIDTH), rb(C_WIDTH),
                  res(wo), res(g), res(wg), res(wu), res(wd), res(fg)],
        out_specs=pl.BlockSpec((tm, D_MODEL), row),
        out_shape=jax.ShapeDtypeStruct((n, D_MODEL), F32),
        scratch_shapes=[pltpu.VMEM((tm, D_MODEL), BF16), pltpu.VMEM((4, tm, LANES), F32)],
        compiler_params=_cparams(("parallel",)),
        name="out_proj_ffn",
    )(x2, ya, ob[0], ob[1], ob[2], lb[0], lb[1], lb[2], yc, wo, g, wg, wu, wd, fg)


def _pad_cols(w, width):
    return jnp.pad(w, ((0, 0), (0, width - w.shape[1])))


def _rope_layout_cols(w):
    z = jnp.zeros(w.shape[:-1] + (32,), w.dtype)
    return jnp.concatenate([w[..., :32], z, w[..., 32:], z], axis=-1)


def _layer_params(l, w_in, a_mu, a_w0, a_decay_up, a_a0, a_iclr_up, a_gate_up, a_k_k, a_k_a,
                  a_r_k, a_ln_w, a_ln_b, c_w_uq, c_w_ukv):
    wi = w_in[l]
    oa = A_PROJ
    ob = oa + 3 * B_WIDTH
    w_all = jnp.concatenate([
        _pad_cols(wi[:, :oa], A_PAD),
        wi[:, oa:ob],
        wi[:, ob:ob + C_Q_LORA + C_KV_LORA],
        _rope_layout_cols(wi[:, ob + C_Q_LORA + C_KV_LORA:]),
    ], axis=1).astype(BF16)
    z64 = jnp.zeros((64, A_WIDTH), F32)
    rw = lambda t: t.reshape(1, -1)
    wq = c_w_uq[l].reshape(C_Q_LORA, C_HEADS, C_NOPE_DIM + C_ROPE_DIM)
    wq = jnp.concatenate([wq[..., :C_NOPE_DIM], _rope_layout_cols(wq[..., C_NOPE_DIM:])], axis=-1)
    wkv = c_w_ukv[l].reshape(C_KV_LORA, C_HEADS, C_NOPE_DIM + C_V_DIM)
    wkv = jnp.concatenate([wkv[..., :C_NOPE_DIM].reshape(C_KV_LORA, -1),
                           wkv[..., C_NOPE_DIM:].reshape(C_KV_LORA, -1)], axis=1)
    return dict(
        w_all=w_all,
        mu=_pad_cols(rw(a_mu[l]), A_PAD), w0=rw(a_w0[l]),
        dup=jnp.concatenate([a_decay_up[l], z64], axis=0),
        a0=rw(a_a0[l]),
        iup=jnp.concatenate([z64, a_iclr_up[l]], axis=0),
        gup=jnp.pad(a_gate_up[l], ((0, A_WIDTH - A_GATE_LORA), (0, 0))),
        k_k=rw(a_k_k[l]), k_a=rw(a_k_a[l]), r_k=rw(a_r_k[l]),
        ln_w=rw(a_ln_w[l]), ln_b=rw(a_ln_b[l]),
        wq=wq.reshape(C_Q_LORA, C_HEADS * C_QK_PAD).astype(BF16),
        wkv=wkv.astype(BF16),
    )


def _rope_tables(positions):
    n = positions.size
    pos = positions.reshape(n, 1).astype(F32)
    inv_freq = lambda dim: 1.0 / (ROPE_THETA ** (jnp.arange(0, dim, 2, dtype=F32) / dim))
    nb_, nc_ = B_ROT_DIM // 2, C_ROPE_DIM // 2
    ang = pos * jnp.concatenate([inv_freq(B_ROT_DIM), inv_freq(C_ROPE_DIM)])
    dense = ang.reshape(-1, LANES)
    cos = jnp.cos(dense).reshape(n, nb_ + nc_)
    sin = jnp.sin(dense).reshape(n, nb_ + nc_)
    cb, sb, cc, sc = cos[:, :nb_], sin[:, :nb_], cos[:, nb_:], sin[:, nb_:]
    one = jnp.ones((n, B_HEAD_DIM - B_ROT_DIM), F32)
    cb = jnp.tile(jnp.concatenate([cb, cb, one], axis=1), (1, LANES // B_HEAD_DIM))
    sb = jnp.tile(jnp.concatenate([-sb, sb, 0.0 * one], axis=1), (1, LANES // B_HEAD_DIM))
    z = jnp.zeros_like(cc)
    cc = jnp.concatenate([cc, z, cc, z], axis=1)
    sc = jnp.concatenate([-sc, z, sc, z], axis=1)
    return cb, sb, cc, sc


def kernel(x, positions, attn_norm_g, w_in, a_mu, a_w0, a_decay_up, a_a0, a_iclr_up, a_gate_up, a_k_k, a_k_a, a_r_k, a_ln_w, a_ln_b, c_q_norm_g, c_kv_norm_g, c_w_uq, c_w_ukv, w_out, ffn_norm_g, ffn_w_gate, ffn_w_up, ffn_w_down, final_norm_g):
    bsz, seq, _ = x.shape
    n = bsz * seq
    depth = w_in.shape[0]
    assert seq % max(w for w, _ in B_PATTERNS) == 0
    cb, sb, cc, sc = _rope_tables(positions)
    x2 = x.reshape(n, D_MODEL)
    tm = min(ROW_TILE, seq)
    for l in range(depth):
        p = _layer_params(l, w_in, a_mu, a_w0, a_decay_up, a_a0, a_iclr_up, a_gate_up, a_k_k,
                          a_k_a, a_r_k, a_ln_w, a_ln_b, c_w_uq, c_w_ukv)
        pa, *qkv = _in_proj(x2, attn_norm_g[l].reshape(1, -1), p["w_all"], cb, sb,
                            c_q_norm_g[l].reshape(1, -1), c_kv_norm_g[l].reshape(1, -1),
                            p["wq"], p["wkv"], cc, sc, bsz, seq, tm)
        qc, kc, vc = qkv[9:]
        ya = _rwkv(pa, p, bsz, seq, min(RWKV_TILE, seq))
        ob, lb = zip(*[_dilated_branch(*qkv[3 * i:3 * i + 3], bsz, seq, d, tm)
                       for i, (_, d) in enumerate(B_PATTERNS)])
        yc = _flash(qc, kc, vc, bsz, seq, min(FLASH_TILE, seq)).reshape(n, C_WIDTH)
        x2 = _ffn(x2, ya, ob, lb, yc, w_out[l].astype(BF16), ffn_norm_g[l].reshape(1, -1),
                  ffn_w_gate[l].astype(BF16), ffn_w_up[l].astype(BF16), ffn_w_down[l].astype(BF16),
                  final_norm_g.reshape(1, -1), tm, l == depth - 1)
    return x2.reshape(bsz, seq, D_MODEL)
```

```python
import functools

import jax
import jax.numpy as jnp
from jax import lax
from jax.experimental import pallas as pl
from jax.experimental.pallas import tpu as pltpu

F32 = jnp.float32
BF16 = jnp.bfloat16

D_MODEL = 1024
NORM_EPS = 1e-6
ROPE_THETA = 500000.0

A_HEADS = 4
A_HEAD_DIM = 64
A_WIDTH = 256
A_DECAY_LORA = 64
A_ICLR_LORA = 64
A_GATE_LORA = 160
A_GN_EPS = 64e-5
A_PROJ = 3 * A_WIDTH + A_DECAY_LORA + A_ICLR_LORA + A_GATE_LORA
A_PAD = 1152
A_CHUNK = 64
RWKV_TILE = 512

B_HEADS = 4
B_HEAD_DIM = 64
B_WIDTH = 256
B_ROT_DIM = 16
B_PATTERNS = ((128, 1), (512, 4), (2048, 16))
B_BLOCK = 128
DIL_UNROLL = 2

C_HEADS = 4
C_NOPE_DIM = 128
C_ROPE_DIM = 64
C_V_DIM = 128
C_Q_LORA = 256
C_KV_LORA = 128
C_WIDTH = 512
C_QK_PAD = 256
C_PAD = 512

MIX_WIDTH = 1024
P_PAD = A_PAD + 3 * B_WIDTH + C_PAD
D_FF = 2816
FF_CHUNK = 256

ROW_TILE = 512
FLASH_TILE = 512
LANES = 128
VMEM_LIMIT = 56 * 1024 * 1024
NEG_BIG = -1e30


def _cparams(sem):
    return pltpu.CompilerParams(dimension_semantics=sem, vmem_limit_bytes=VMEM_LIMIT)


def _rms(x, g):
    return x * lax.rsqrt(jnp.mean(x * x, axis=-1, keepdims=True) + NORM_EPS) * g


def _dot(a, b):
    return jnp.dot(a.astype(BF16), b.astype(BF16), preferred_element_type=F32)


def _dot_nt(a, b):
    return lax.dot_general(a.astype(BF16), b.astype(BF16), (((1,), (1,)), ((), ())),
                           preferred_element_type=F32)


def _dot_tn(a, b):
    return lax.dot_general(a.astype(BF16), b.astype(BF16), (((0,), (0,)), ((), ())),
                           preferred_element_type=F32)


def _split3(x):
    hi = x.astype(BF16)
    r1 = x - hi.astype(F32)
    mid = r1.astype(BF16)
    lo = (r1 - mid.astype(F32)).astype(BF16)
    return hi, mid, lo


def _dot_hi(a, b):
    ah = a.astype(BF16)
    al = (a - ah.astype(F32)).astype(BF16)
    bh = b.astype(BF16)
    bl = (b - bh.astype(F32)).astype(BF16)
    return (jnp.dot(ah, bh, preferred_element_type=F32)
            + jnp.dot(al, bh, preferred_element_type=F32)
            + jnp.dot(ah, bl, preferred_element_type=F32))


def _sigmoid(x):
    return 1.0 / (1.0 + jnp.exp(-x))


def _in_proj_body(x_ref, g_ref, w_ref, cb_ref, sb_ref, qg_ref, kvg_ref, wq_ref, wkv_ref, cc_ref, sc_ref,
                  pa_ref, q1_ref, k1_ref, v1_ref, q4_ref, k4_ref, v4_ref, q16_ref, k16_ref, v16_ref,
                  qc_ref, kc_ref, vc_ref, slab_s):
    tm = x_ref.shape[0]
    h = _rms(x_ref[...], g_ref[...])
    y = jnp.dot(h.astype(BF16), w_ref[...], preferred_element_type=F32)
    pa_ref[...] = y[:, :A_PAD]
    cb = jnp.concatenate([cb_ref[...]] * 2, axis=1)
    sb = jnp.concatenate([sb_ref[...]] * 2, axis=1)
    lane = lax.broadcasted_iota(jnp.int32, (1, B_WIDTH), 1)
    first = (lane % B_HEAD_DIM) < (B_ROT_DIM // 2)

    def rope(t):
        partner = jnp.where(first, pltpu.roll(t, B_WIDTH - B_ROT_DIM // 2, 1),
                            pltpu.roll(t, B_ROT_DIM // 2, 1))
        return t * cb + partner * sb

    o = A_PAD
    qkv = (rope(y[:, o:o + B_WIDTH]) * (B_HEAD_DIM ** -0.5),
           rope(y[:, o + B_WIDTH:o + 2 * B_WIDTH]),
           y[:, o + 2 * B_WIDTH:o + 3 * B_WIDTH])
    _mla_up(y[:, o + 3 * B_WIDTH:], qg_ref, kvg_ref, wq_ref, wkv_ref, cc_ref, sc_ref,
            qc_ref, kc_ref, vc_ref)
    for i, (t, o_ref) in enumerate(zip(qkv, (q1_ref, k1_ref, v1_ref))):
        o_ref[...] = t.astype(BF16)
        slab_s[2 * i] = t[:, :LANES]
        slab_s[2 * i + 1] = t[:, LANES:]
    for dil, outs in ((4, (q4_ref, k4_ref, v4_ref)), (16, (q16_ref, k16_ref, v16_ref))):
        gs = tm // dil
        for i, o_ref in enumerate(outs):
            for r in range(dil):
                rows = pl.ds(r, gs, stride=dil)
                o_ref[r * gs:(r + 1) * gs, :] = jnp.concatenate(
                    [slab_s[2 * i, rows, :], slab_s[2 * i + 1, rows, :]], axis=1).astype(BF16)


def _mla_up(pc, qg_ref, kvg_ref, wq_ref, wkv_ref, cc_ref, sc_ref, q_ref, k_ref, v_ref):
    cq = _rms(pc[:, :C_Q_LORA], qg_ref[...])
    ckv = _rms(pc[:, C_Q_LORA:C_Q_LORA + C_KV_LORA], kvg_ref[...])
    kr = pc[:, C_Q_LORA + C_KV_LORA:]
    scale = (C_NOPE_DIM + C_ROPE_DIM) ** -0.5
    q = jnp.dot(cq.astype(BF16), wq_ref[...], preferred_element_type=F32) * scale
    kv = jnp.dot(ckv.astype(BF16), wkv_ref[...], preferred_element_type=F32)
    cc = cc_ref[...]
    sc = sc_ref[...]

    def rope(t):
        return t * cc + pltpu.roll(t, 64, 1) * sc

    krr = rope(kr).astype(BF16)
    for h in range(C_HEADS):
        o = h * C_QK_PAD
        q_ref[0, h, :, 0:C_NOPE_DIM] = q[:, o:o + C_NOPE_DIM].astype(BF16)
        q_ref[0, h, :, C_NOPE_DIM:] = rope(q[:, o + C_NOPE_DIM:o + C_QK_PAD]).astype(BF16)
        k_ref[0, h, :, 0:C_NOPE_DIM] = kv[:, h * C_NOPE_DIM:(h + 1) * C_NOPE_DIM].astype(BF16)
        k_ref[0, h, :, C_NOPE_DIM:] = krr
        vo = C_HEADS * C_NOPE_DIM + h * C_V_DIM
        v_ref[0, h, :, 0:C_V_DIM] = kv[:, vo:vo + C_V_DIM].astype(BF16)
        v_ref[0, h, :, C_V_DIM:] = jnp.ones((kv.shape[0], C_V_DIM), BF16)


def _in_proj(x2, g, w, cb, sb, qg, kvg, wq, wkv, cc, sc, bsz, seq, tm):
    n = x2.shape[0]
    nt = seq // tm
    row = lambda i: (i, 0)
    rb = lambda wd: pl.BlockSpec((tm, wd), row)
    res = lambda a: pl.BlockSpec(a.shape, lambda i: (0,) * a.ndim, pipeline_mode=pl.Buffered(1))
    hm = lambda wd: pl.BlockSpec((1, C_HEADS, tm, wd), lambda i: (i // nt, 0, i % nt, 0))
    bsd = jax.ShapeDtypeStruct((n, B_WIDTH), BF16)
    hsd = lambda wd: jax.ShapeDtypeStruct((bsz, C_HEADS, seq, wd), BF16)
    return pl.pallas_call(
        _in_proj_body,
        grid=(n // tm,),
        in_specs=[rb(D_MODEL), res(g), res(w), rb(LANES), rb(LANES),
                  res(qg), res(kvg), res(wq), res(wkv), rb(LANES), rb(LANES)],
        out_specs=[rb(A_PAD)] + [rb(B_WIDTH)] * 9 + [hm(C_QK_PAD), hm(C_QK_PAD), hm(2 * C_V_DIM)],
        out_shape=[jax.ShapeDtypeStruct((n, A_PAD), F32)] + [bsd] * 9
                  + [hsd(C_QK_PAD), hsd(C_QK_PAD), hsd(2 * C_V_DIM)],
        scratch_shapes=[pltpu.VMEM((6, tm, LANES), F32)],
        compiler_params=_cparams(("parallel",)),
        name="in_proj",
    )(x2, g, w, cb, sb, qg, kvg, wq, wkv, cc, sc)


def _rwkv_body(pa_ref, mu_ref, w0_ref, dup_ref, a0_ref, iup_ref, gup_ref, kk_ref, ka_ref,
               rk_ref, lnw_ref, lnb_ref, out_ref, prev_s, h_s, *, ts):
    T = A_CHUNK
    HT = A_HEADS * T
    W = A_WIDTH
    C = ts // T

    @pl.when(pl.program_id(1) == 0)
    def _():
        prev_s[...] = jnp.zeros_like(prev_s)
        h_s[...] = jnp.zeros_like(h_s)

    lane_w = lax.broadcasted_iota(jnp.int32, (W, W), 1)
    row_w = lax.broadcasted_iota(jnp.int32, (W, W), 0)
    ebd = jnp.where(lane_w // A_HEAD_DIM == row_w // A_HEAD_DIM, 1.0, 0.0).astype(F32)

    pa = pa_ref[...]
    rid = lax.broadcasted_iota(jnp.int32, (ts, 1), 0)
    shifted = jnp.where(rid == 0, prev_s[0:1, :], pltpu.roll(pa, 1, 0))
    prev_s[0:1, :] = pa[ts - 1:ts, :]
    pf = pa + (shifted - pa) * mu_ref[...]
    r = pf[:, 0:W]
    k = pf[:, W:2 * W]
    v = pf[:, 2 * W:3 * W]
    xwa = pf[:, 3 * W:3 * W + LANES]
    xg = pf[:, 3 * W + LANES:]
    dl = _dot_hi(jnp.tanh(xwa), dup_ref[...])
    z = -(w0_ref[...] + dl)
    softplus = jnp.maximum(z, 0.0) + jnp.log(1.0 + jnp.exp(-jnp.abs(z)))
    w_log = -softplus - 0.5
    lw = -jnp.exp(w_log)
    a = _sigmoid(a0_ref[...] + _dot(xwa, iup_ref[...]))
    g = _dot(_sigmoid(xg), gup_ref[...])
    kkv = k * kk_ref[...]
    ss = _dot(kkv * kkv, ebd)
    kn = kkv / jnp.maximum(jnp.sqrt(ss), 1e-12)
    k2 = k * (1.0 + (a - 1.0) * ka_ref[...])
    bon = _dot(r * k2 * rk_ref[...], ebd) * v
    b = kn * a

    tt = lax.broadcasted_iota(jnp.int32, (T, HT), 0)
    ss = lax.broadcasted_iota(jnp.int32, (T, HT), 1) % T
    strict = tt > ss
    lower = tt >= ss
    rr = lax.broadcasted_iota(jnp.int32, (HT, HT), 0)
    cc = lax.broadcasted_iota(jnp.int32, (HT, HT), 1)
    eye_w = rr == cc
    hmask = (rr // T) == (cc // A_HEAD_DIM)
    tr = lax.broadcasted_iota(jnp.int32, (T, T), 0)
    tc = lax.broadcasted_iota(jnp.int32, (T, T), 1)
    ltri = jnp.where(tr >= tc, 1.0, 0.0).astype(BF16)

    def stack(x):
        reps = x.shape[1] // W
        mask = hmask if reps == 1 else jnp.concatenate([hmask] * reps, axis=1)
        return jnp.where(mask, jnp.concatenate([x] * A_HEADS, axis=0), 0.0).astype(BF16)

    def chunks(x):
        return [x[c * T:(c + 1) * T, :] for c in range(C)]

    def each(fn, *lists):
        return [fn(*xs) for xs in zip(*lists)]

    rc, kc, vc, knc, bc, lwc = (chunks(t) for t in (r, k2, v, kn, b, lw))
    cum = each(lambda x: _cumsum_rows(x, ltri), lwc)
    e_in = each(jnp.exp, cum)
    rt = each(lambda x, e: x * e, rc, e_in)
    kt = each(lambda x, cu, l: x * jnp.exp(cu - l), knc, cum, lwc)
    e_out = each(lambda cu: jnp.exp(-cu), cum)
    bhs = each(lambda x, e: stack(x * e), bc, e_out)
    khs = each(lambda x, e: stack(x * e), kc, e_out)
    e_end = each(lambda cu: jnp.exp(cu[T - 1:T, :] - cu), cum)
    bg = each(lambda x, e: x * e, bc, e_end)
    kg = each(lambda x, e: x * e, kc, e_end)
    vs = each(stack, vc)

    a4 = each(lambda k_, r_, b_, kh_: _dot_nt(jnp.concatenate([k_, r_], axis=0),
                                              jnp.concatenate([b_, kh_], axis=0)), kt, rt, bhs, khs)
    a_ab = each(lambda t: jnp.where(strict, t[:T, :HT], 0.0), a4)
    a_akrk = each(lambda t: jnp.concatenate([jnp.where(strict, t[:T, HT:], 0.0),
                                             jnp.where(lower, t[T:, HT:], 0.0)], axis=0), a4)
    a_rb = each(lambda t: jnp.where(lower, t[T:, :HT], 0.0), a4)

    m = each(lambda t: jnp.where(tt == ss, 1.0, 0.0) - jnp.where((tt // 2 == ss // 2), t, 0.0), a_ab)
    s = 2
    while s < T:
        blk = (tt // (2 * s) == ss // (2 * s)) & ((tt // s) % 2 == 1) & ((ss // s) % 2 == 0)
        mc = each(lambda m_, t: _dot(m_, stack(jnp.where(blk, t, 0.0))), m, a_ab)
        m = each(lambda m_, t: m_ - _dot(t, stack(m_)), m, mc)
        s *= 2

    x1 = each(_dot, a_akrk, vs)
    pq = each(lambda m_, k_, x: _dot(m_, stack(jnp.concatenate([k_, x[:T]], axis=1))), m, kt, x1)
    x2 = each(lambda a_, t: _dot(a_, stack(t)), a_rb, pq)
    rp = each(lambda r_, x: r_ - x[:, :W], rt, x2)
    y0 = each(lambda x1_, x: x1_[T:] - x[:, W:], x1, x2)
    gd = each(_dot_tn, bg, pq)
    kv = each(_dot_tn, kg, vc)
    gmat = each(lambda e, t: jnp.where(eye_w, e[T - 1:T, :], 0.0) - jnp.where(hmask, t[:, :W], 0.0),
                e_in, gd)
    dmat = each(lambda kv_, t: jnp.where(hmask, kv_ - t[:, W:], 0.0), kv, gd)

    h = h_s[...]
    ys = []
    for c in range(C):
        t = _dot(jnp.concatenate([rp[c], gmat[c]], axis=0), h)
        ys.append(t[:T] + y0[c])
        h = t[T:] + dmat[c]
    h_s[...] = h
    y = jnp.concatenate(ys, axis=0)

    inv_n = 1.0 / A_HEAD_DIM
    mean = _dot(y, ebd) * inv_n
    yc = y - mean
    var = _dot(yc * yc, ebd) * inv_n
    yn = yc * lax.rsqrt(var + A_GN_EPS) * lnw_ref[...] + lnb_ref[...]
    out_ref[...] = (yn + bon) * g


def _cumsum_rows(x, ltri):
    hi, mid, lo = _split3(x)
    return (jnp.dot(ltri, hi, preferred_element_type=F32)
            + jnp.dot(ltri, mid, preferred_element_type=F32)
            + jnp.dot(ltri, lo, preferred_element_type=F32))


def _rwkv(pa, p, bsz, seq, ts):
    nt = seq // ts
    row = lambda b, i: (b * nt + i, 0)
    const = lambda b, i: (0, 0)
    vec = lambda w: pl.BlockSpec((1, w), const)
    return pl.pallas_call(
        functools.partial(_rwkv_body, ts=ts),
        grid=(bsz, nt),
        in_specs=[pl.BlockSpec((ts, A_PAD), row), vec(A_PAD), vec(A_WIDTH),
                  pl.BlockSpec((A_DECAY_LORA + A_ICLR_LORA, A_WIDTH), const), vec(A_WIDTH),
                  pl.BlockSpec((A_DECAY_LORA + A_ICLR_LORA, A_WIDTH), const),
                  pl.BlockSpec((A_WIDTH, A_WIDTH), const),
                  vec(A_WIDTH), vec(A_WIDTH), vec(A_WIDTH), vec(A_WIDTH), vec(A_WIDTH)],
        out_specs=pl.BlockSpec((ts, A_WIDTH), row),
        out_shape=jax.ShapeDtypeStruct((bsz * seq, A_WIDTH), F32),
        scratch_shapes=[pltpu.VMEM((8, A_PAD), F32), pltpu.VMEM((A_WIDTH, A_WIDTH), F32)],
        compiler_params=_cparams(("parallel", "arbitrary")),
        name="rwkv7",
    )(pa, p["mu"], p["w0"], p["dup"], p["a0"], p["iup"], p["gup"], p["k_k"], p["k_a"],
      p["r_k"], p["ln_w"], p["ln_b"])


def _dilated_body(q_ref, k_ref, v_ref, o_ref, l_ref, *, nb, gs, dil):
    Q = B_BLOCK
    HQ = B_HEADS * Q
    W = B_WIDTH
    rr = lax.broadcasted_iota(jnp.int32, (HQ, W), 0)
    cc = lax.broadcasted_iota(jnp.int32, (HQ, W), 1)
    hsel = jnp.where((rr // Q) == (cc // B_HEAD_DIM), 1.0, 0.0).astype(BF16)
    qi = rr % Q
    dist = qi + Q - cc
    band2 = (dist >= 0) & (dist <= Q)
    lane_h = lax.broadcasted_iota(jnp.int32, (Q, W), 1) // B_HEAD_DIM

    def where(n):
        if gs >= Q:
            per = gs // Q
            return n // per, pl.ds(pl.multiple_of((n % per) * Q, Q), Q)
        per = Q // gs
        return pl.ds(pl.multiple_of(n * per, per), per), slice(None)

    def load(ref, r, n):
        g, rows = where(n)
        return ref[g, r, rows, :].reshape(Q, W)

    def store(ref, r, n, val):
        g, rows = where(n)
        ref[g, r, rows, :] = val.reshape(ref[g, r, rows, :].shape)

    def each(fn, *lists):
        return [fn(*xs) for xs in zip(*lists)]

    def attend(r, ns):
        first = [n == 0 for n in ns]
        prev = [jnp.maximum(n - 1, 0) for n in ns]
        qs = [jnp.concatenate([load(q_ref, r, n)] * B_HEADS, axis=0) * hsel for n in ns]
        kw = [jnp.concatenate([load(k_ref, r, p), load(k_ref, r, n)], axis=0) for p, n in zip(prev, ns)]
        vw = [jnp.concatenate([load(v_ref, r, p), load(v_ref, r, n)], axis=0) for p, n in zip(prev, ns)]
        s = each(lambda a, b: lax.dot_general(a, b, (((1,), (1,)), ((), ())),
                                              preferred_element_type=F32), qs, kw)
        s = each(lambda t, f: jnp.where(band2 & ((cc >= Q) | jnp.logical_not(f)), t, NEG_BIG), s, first)
        m = each(lambda t: jnp.max(t, axis=-1, keepdims=True), s)
        p = each(lambda t, m_: jnp.exp(t - m_), s, m)
        l = each(lambda t: jnp.sum(t, axis=-1, keepdims=True), p)
        o = each(lambda p_, v_, l_: jnp.dot(p_.astype(BF16), v_, preferred_element_type=F32) / l_,
                 p, vw, l)
        lse = each(lambda m_, l_: jnp.broadcast_to(m_ + jnp.log(l_), (HQ, W)), m, l)
        for n, o_, lse_ in zip(ns, o, lse):
            out = o_[0:Q]
            lout = lse_[0:Q]
            for h in range(1, B_HEADS):
                out = jnp.where(lane_h == h, o_[h * Q:(h + 1) * Q], out)
                lout = jnp.where(lane_h == h, lse_[h * Q:(h + 1) * Q], lout)
            store(o_ref, r, n, out.astype(o_ref.dtype))
            store(l_ref, r, n, lout)

    unroll = min(DIL_UNROLL, nb)
    per_res = nb // unroll

    def body(t, carry):
        t0 = (t % per_res) * unroll
        attend(t // per_res, [t0 + i for i in range(unroll)])
        return carry

    lax.fori_loop(0, dil * per_res, body, 0)


def _dilated_branch(q, k, v, bsz, seq, dil, tm):
    gs = tm // dil
    nt = seq // tm
    view = lambda t: t.reshape(bsz, nt, dil, gs, B_WIDTH)
    spec = pl.BlockSpec((None, nt, dil, gs, B_WIDTH), lambda b: (b, 0, 0, 0, 0))
    o, l = pl.pallas_call(
        functools.partial(_dilated_body, nb=seq // dil // B_BLOCK, gs=gs, dil=dil),
        grid=(bsz,),
        in_specs=[spec, spec, spec],
        out_specs=[spec, spec],
        out_shape=[jax.ShapeDtypeStruct((bsz, nt, dil, gs, B_WIDTH), BF16),
                   jax.ShapeDtypeStruct((bsz, nt, dil, gs, B_WIDTH), F32)],
        compiler_params=_cparams(("parallel",)),
        name="dilated_d%d" % dil,
    )(view(q), view(k), view(v))
    return o.reshape(bsz * seq, B_WIDTH), l.reshape(bsz * seq, B_WIDTH)


def _flash_body(q_ref, k_ref, v_ref, o_ref, *, tq, nq):
    hq = tq // 2
    causal_top = (lax.broadcasted_iota(jnp.int32, (hq, hq), 1)
                  <= lax.broadcasted_iota(jnp.int32, (hq, hq), 0))
    causal_bot = (lax.broadcasted_iota(jnp.int32, (hq, tq), 1)
                  <= lax.broadcasted_iota(jnp.int32, (hq, tq), 0) + hq)
    blocks = [(i, j) for i in range(nq) for j in range(i + 1)]

    def qk(rows, keys):
        return lax.dot_general(q_ref[0, 0, rows, :], k_ref[0, 0, keys, :],
                               (((1,), (1,)), ((), ())), preferred_element_type=F32)

    def scores(i, j):
        if j < i:
            return (qk(slice(i * tq, (i + 1) * tq), slice(j * tq, (j + 1) * tq)),)
        return (qk(slice(i * tq, i * tq + hq), slice(j * tq, j * tq + hq)),
                qk(slice(i * tq + hq, (i + 1) * tq), slice(j * tq, (j + 1) * tq)))

    def update(m, l, acc, s, vb):
        m_new = jnp.maximum(m, jnp.max(s, axis=-1, keepdims=True))
        alpha = jnp.exp(m - m_new)
        p = jnp.exp(s - jnp.concatenate([m_new] * (s.shape[1] // LANES), axis=1))
        pv = jnp.dot(p.astype(BF16), vb, preferred_element_type=F32)
        return m_new, alpha * l + pv[:, C_V_DIM:], alpha * acc + pv[:, :C_V_DIM]

    s_next = scores(*blocks[0])
    m = l = acc = None
    for idx, (i, j) in enumerate(blocks):
        s = s_next
        if idx + 1 < len(blocks):
            s_next = scores(*blocks[idx + 1])
        if j == 0:
            m = jnp.full((tq, C_V_DIM), NEG_BIG, F32)
            l = jnp.zeros((tq, C_V_DIM), F32)
            acc = jnp.zeros((tq, C_V_DIM), F32)
        if j < i:
            m, l, acc = update(m, l, acc, s[0], v_ref[0, 0, j * tq:(j + 1) * tq, :])
        else:
            top = update(m[:hq], l[:hq], acc[:hq], jnp.where(causal_top, s[0], NEG_BIG),
                         v_ref[0, 0, j * tq:j * tq + hq, :])
            bot = update(m[hq:], l[hq:], acc[hq:], jnp.where(causal_bot, s[1], NEG_BIG),
                         v_ref[0, 0, j * tq:(j + 1) * tq, :])
            o_ref[0, i * tq:i * tq + hq, :] = top[2] / top[1]
            o_ref[0, i * tq + hq:(i + 1) * tq, :] = bot[2] / bot[1]


def _flash(q, k, v, bsz, seq, tq):
    nq = seq // tq
    hb = lambda w: pl.BlockSpec((1, 1, seq, w), lambda b, h: (b, h, 0, 0))
    return pl.pallas_call(
        functools.partial(_flash_body, tq=tq, nq=nq),
        grid=(bsz, C_HEADS),
        in_specs=[hb(C_QK_PAD), hb(C_QK_PAD), hb(2 * C_V_DIM)],
        out_specs=pl.BlockSpec((1, seq, C_V_DIM), lambda b, h: (b, 0, h)),
        out_shape=jax.ShapeDtypeStruct((bsz, seq, C_WIDTH), F32),
        compiler_params=_cparams(("parallel", "parallel")),
        name="mla_flash",
    )(q, k, v)


def _ffn_body(x_ref, ya_ref, o1_ref, o2_ref, o3_ref, l1_ref, l2_ref, l3_ref, yc_ref,
              wo_ref, g_ref, wg_ref, wu_ref, wd_ref, fg_ref, out_ref, h_s, slab_s, *, final):
    tm = x_ref.shape[0]

    def natural(dil, o_ref, l_ref):
        gs = tm // dil
        for r in range(dil):
            rows = pl.ds(r, gs, stride=dil)
            ov = o_ref[r * gs:(r + 1) * gs, :].astype(F32)
            lv = l_ref[r * gs:(r + 1) * gs, :]
            slab_s[0, rows, :] = ov[:, :LANES]
            slab_s[1, rows, :] = ov[:, LANES:]
            slab_s[2, rows, :] = lv[:, :LANES]
            slab_s[3, rows, :] = lv[:, LANES:]
        return (jnp.concatenate([slab_s[0], slab_s[1]], axis=1),
                jnp.concatenate([slab_s[2], slab_s[3]], axis=1))

    mix = (jnp.dot(ya_ref[...].astype(BF16), wo_ref[0:A_WIDTH, :], preferred_element_type=F32)
           + jnp.dot(yc_ref[...].astype(BF16), wo_ref[A_WIDTH + B_WIDTH:, :],
                     preferred_element_type=F32))
    o1, l1 = o1_ref[...].astype(F32), l1_ref[...]
    o2, l2 = natural(B_PATTERNS[1][1], o2_ref, l2_ref)
    o3, l3 = natural(B_PATTERNS[2][1], o3_ref, l3_ref)
    mx = jnp.maximum(jnp.maximum(l1, l2), l3)
    e1, e2, e3 = jnp.exp(l1 - mx), jnp.exp(l2 - mx), jnp.exp(l3 - mx)
    yb = (e1 * o1 + e2 * o2 + e3 * o3) / (e1 + e2 + e3)
    mix = mix + jnp.dot(yb.astype(BF16), wo_ref[A_WIDTH:A_WIDTH + B_WIDTH, :],
                        preferred_element_type=F32)
    xn = x_ref[...] + mix
    out_ref[...] = xn
    h_s[...] = _rms(xn, g_ref[...]).astype(BF16)

    for j in range(D_FF // FF_CHUNK):
        cols = slice(j * FF_CHUNK, (j + 1) * FF_CHUNK)
        h = h_s[...]
        gt = jnp.dot(h, wg_ref[:, cols], preferred_element_type=F32)
        up = jnp.dot(h, wu_ref[:, cols], preferred_element_type=F32)
        act = gt * _sigmoid(gt) * up
        out_ref[...] += jnp.dot(act.astype(BF16), wd_ref[cols, :], preferred_element_type=F32)
    if final:
        out_ref[...] = _rms(out_ref[...], fg_ref[...])


def _ffn(x2, ya, ob, lb, yc, wo, g, wg, wu, wd, fg, tm, final):
    n = x2.shape[0]
    row = lambda i: (i, 0)
    rb = lambda w: pl.BlockSpec((tm, w), row)
    res = lambda a: pl.BlockSpec(a.shape, lambda i: (0,) * a.ndim, pipeline_mode=pl.Buffered(1))
    return pl.pallas_call(
        functools.partial(_ffn_body, final=final),
        grid=(n // tm,),
        in_specs=[rb(D_MODEL), rb(A_WIDTH), rb(B_WIDTH), rb(B_WIDTH), rb(B_WIDTH),
                  rb(B_WIDTH), rb(B_WIDTH), rb(B_WIDTH), rb(C_WIDTH),
                  res(wo), res(g), res(wg), res(wu), res(wd), res(fg)],
        out_specs=pl.BlockSpec((tm, D_MODEL), row),
        out_shape=jax.ShapeDtypeStruct((n, D_MODEL), F32),
        scratch_shapes=[pltpu.VMEM((tm, D_MODEL), BF16), pltpu.VMEM((4, tm, LANES), F32)],
        compiler_params=_cparams(("parallel",)),
        name="out_proj_ffn",
    )(x2, ya, ob[0], ob[1], ob[2], lb[0], lb[1], lb[2], yc, wo, g, wg, wu, wd, fg)


def _pad_cols(w, width):
    return jnp.pad(w, ((0, 0), (0, width - w.shape[1])))


def _rope_layout_cols(w):
    z = jnp.zeros(w.shape[:-1] + (32,), w.dtype)
    return jnp.concatenate([w[..., :32], z, w[..., 32:], z], axis=-1)


def _layer_params(l, w_in, a_mu, a_w0, a_decay_up, a_a0, a_iclr_up, a_gate_up, a_k_k, a_k_a,
                  a_r_k, a_ln_w, a_ln_b, c_w_uq, c_w_ukv):
    wi = w_in[l]
    oa = A_PROJ
    ob = oa + 3 * B_WIDTH
    w_all = jnp.concatenate([
        _pad_cols(wi[:, :oa], A_PAD),
        wi[:, oa:ob],
        wi[:, ob:ob + C_Q_LORA + C_KV_LORA],
        _rope_layout_cols(wi[:, ob + C_Q_LORA + C_KV_LORA:]),
    ], axis=1).astype(BF16)
    z64 = jnp.zeros((64, A_WIDTH), F32)
    rw = lambda t: t.reshape(1, -1)
    wq = c_w_uq[l].reshape(C_Q_LORA, C_HEADS, C_NOPE_DIM + C_ROPE_DIM)
    wq = jnp.concatenate([wq[..., :C_NOPE_DIM], _rope_layout_cols(wq[..., C_NOPE_DIM:])], axis=-1)
    wkv = c_w_ukv[l].reshape(C_KV_LORA, C_HEADS, C_NOPE_DIM + C_V_DIM)
    wkv = jnp.concatenate([wkv[..., :C_NOPE_DIM].reshape(C_KV_LORA, -1),
                           wkv[..., C_NOPE_DIM:].reshape(C_KV_LORA, -1)], axis=1)
    return dict(
        w_all=w_all,
        mu=_pad_cols(rw(a_mu[l]), A_PAD), w0=rw(a_w0[l]),
        dup=jnp.concatenate([a_decay_up[l], z64], axis=0),
        a0=rw(a_a0[l]),
        iup=jnp.concatenate([z64, a_iclr_up[l]], axis=0),
        gup=jnp.pad(a_gate_up[l], ((0, A_WIDTH - A_GATE_LORA), (0, 0))),
        k_k=rw(a_k_k[l]), k_a=rw(a_k_a[l]), r_k=rw(a_r_k[l]),
        ln_w=rw(a_ln_w[l]), ln_b=rw(a_ln_b[l]),
        wq=wq.reshape(C_Q_LORA, C_HEADS * C_QK_PAD).astype(BF16),
        wkv=wkv.astype(BF16),
    )


def _rope_tables(positions):
    n = positions.size
    pos = positions.reshape(n, 1).astype(F32)

    def tables(dim):
        inv_freq = 1.0 / (ROPE_THETA ** (jnp.arange(0, dim, 2, dtype=F32) / dim))
        ang = pos * inv_freq
        return jnp.cos(ang), jnp.sin(ang)

    cb, sb = tables(B_ROT_DIM)
    one = jnp.ones((n, B_HEAD_DIM - B_ROT_DIM), F32)
    cb = jnp.tile(jnp.concatenate([cb, cb, one], axis=1), (1, LANES // B_HEAD_DIM))
    sb = jnp.tile(jnp.concatenate([-sb, sb, 0.0 * one], axis=1), (1, LANES // B_HEAD_DIM))
    cc, sc = tables(C_ROPE_DIM)
    z = jnp.zeros_like(cc)
    cc = jnp.concatenate([cc, z, cc, z], axis=1)
    sc = jnp.concatenate([-sc, z, sc, z], axis=1)
    return cb, sb, cc, sc


def kernel(x, positions, attn_norm_g, w_in, a_mu, a_w0, a_decay_up, a_a0, a_iclr_up, a_gate_up, a_k_k, a_k_a, a_r_k, a_ln_w, a_ln_b, c_q_norm_g, c_kv_norm_g, c_w_uq, c_w_ukv, w_out, ffn_norm_g, ffn_w_gate, ffn_w_up, ffn_w_down, final_norm_g):
    bsz, seq, _ = x.shape
    n = bsz * seq
    depth = w_in.shape[0]
    assert seq % max(w for w, _ in B_PATTERNS) == 0
    cb, sb, cc, sc = _rope_tables(positions)
    x2 = x.reshape(n, D_MODEL)
    tm = min(ROW_TILE, seq)
    for l in range(depth):
        p = _layer_params(l, w_in, a_mu, a_w0, a_decay_up, a_a0, a_iclr_up, a_gate_up, a_k_k,
                          a_k_a, a_r_k, a_ln_w, a_ln_b, c_w_uq, c_w_ukv)
        pa, *qkv = _in_proj(x2, attn_norm_g[l].reshape(1, -1), p["w_all"], cb, sb,
                            c_q_norm_g[l].reshape(1, -1), c_kv_norm_g[l].reshape(1, -1),
                            p["wq"], p["wkv"], cc, sc, bsz, seq, tm)
        qc, kc, vc = qkv[9:]
        ya = _rwkv(pa, p, bsz, seq, min(RWKV_TILE, seq))
        ob, lb = zip(*[_dilated_branch(*qkv[3 * i:3 * i + 3], bsz, seq, d, tm)
                       for i, (_, d) in enumerate(B_PATTERNS)])
        yc = _flash(qc, kc, vc, bsz, seq, min(FLASH_TILE, seq)).reshape(n, C_WIDTH)
        x2 = _ffn(x2, ya, ob, lb, yc, w_out[l].astype(BF16), ffn_norm_g[l].reshape(1, -1),
                  ffn_w_gate[l].astype(BF16), ffn_w_up[l].astype(BF16), ffn_w_down[l].astype(BF16),
                  final_norm_g.reshape(1, -1), tm, l == depth - 1)
    return x2.reshape(bsz, seq, D_MODEL)
```

```python
import functools

import jax
import jax.numpy as jnp
from jax import lax
from jax.experimental import pallas as pl
from jax.experimental.pallas import tpu as pltpu

F32 = jnp.float32
BF16 = jnp.bfloat16

D_MODEL = 1024
NORM_EPS = 1e-6
ROPE_THETA = 500000.0

A_HEADS = 4
A_HEAD_DIM = 64
A_WIDTH = 256
A_DECAY_LORA = 64
A_ICLR_LORA = 64
A_GATE_LORA = 160
A_GN_EPS = 64e-5
A_PROJ = 3 * A_WIDTH + A_DECAY_LORA + A_ICLR_LORA + A_GATE_LORA
A_PAD = 1152
A_CHUNK = 64
RWKV_TILE = 512

B_HEADS = 4
B_HEAD_DIM = 64
B_WIDTH = 256
B_ROT_DIM = 16
B_PATTERNS = ((128, 1), (512, 4), (2048, 16))
B_BLOCK = 128
DIL_UNROLL = 1

C_HEADS = 4
C_NOPE_DIM = 128
C_ROPE_DIM = 64
C_V_DIM = 128
C_Q_LORA = 256
C_KV_LORA = 128
C_WIDTH = 512
C_QK_PAD = 256
C_PAD = 512

MIX_WIDTH = 1024
P_PAD = A_PAD + 3 * B_WIDTH + C_PAD
D_FF = 2816
FF_CHUNK = 256

ROW_TILE = 512
FLASH_TILE = 512
LANES = 128
VMEM_LIMIT = 56 * 1024 * 1024
NEG_BIG = -1e30


def _cparams(sem):
    return pltpu.CompilerParams(dimension_semantics=sem, vmem_limit_bytes=VMEM_LIMIT)


def _rms(x, g):
    return x * lax.rsqrt(jnp.mean(x * x, axis=-1, keepdims=True) + NORM_EPS) * g


def _dot(a, b):
    return jnp.dot(a.astype(BF16), b.astype(BF16), preferred_element_type=F32)


def _dot_nt(a, b):
    return lax.dot_general(a.astype(BF16), b.astype(BF16), (((1,), (1,)), ((), ())),
                           preferred_element_type=F32)


def _dot_tn(a, b):
    return lax.dot_general(a.astype(BF16), b.astype(BF16), (((0,), (0,)), ((), ())),
                           preferred_element_type=F32)


def _split3(x):
    hi = x.astype(BF16)
    r1 = x - hi.astype(F32)
    mid = r1.astype(BF16)
    lo = (r1 - mid.astype(F32)).astype(BF16)
    return hi, mid, lo


def _dot_hi(a, b):
    ah = a.astype(BF16)
    al = (a - ah.astype(F32)).astype(BF16)
    bh = b.astype(BF16)
    bl = (b - bh.astype(F32)).astype(BF16)
    return (jnp.dot(ah, bh, preferred_element_type=F32)
            + jnp.dot(al, bh, preferred_element_type=F32)
            + jnp.dot(ah, bl, preferred_element_type=F32))


def _sigmoid(x):
    return 1.0 / (1.0 + jnp.exp(-x))


def _in_proj_body(x_ref, g_ref, w_ref, cb_ref, sb_ref, qg_ref, kvg_ref, wq_ref, wkv_ref, cc_ref, sc_ref,
                  pa_ref, q1_ref, k1_ref, v1_ref, q4_ref, k4_ref, v4_ref, q16_ref, k16_ref, v16_ref,
                  qc_ref, kc_ref, vc_ref, slab_s):
    tm = x_ref.shape[0]
    h = _rms(x_ref[...], g_ref[...])
    y = jnp.dot(h.astype(BF16), w_ref[...], preferred_element_type=F32)
    pa_ref[...] = y[:, :A_PAD]
    cb = jnp.concatenate([cb_ref[...]] * 2, axis=1)
    sb = jnp.concatenate([sb_ref[...]] * 2, axis=1)
    lane = lax.broadcasted_iota(jnp.int32, (1, B_WIDTH), 1)
    first = (lane % B_HEAD_DIM) < (B_ROT_DIM // 2)

    def rope(t):
        partner = jnp.where(first, pltpu.roll(t, B_WIDTH - B_ROT_DIM // 2, 1),
                            pltpu.roll(t, B_ROT_DIM // 2, 1))
        return t * cb + partner * sb

    o = A_PAD
    qkv = (rope(y[:, o:o + B_WIDTH]) * (B_HEAD_DIM ** -0.5),
           rope(y[:, o + B_WIDTH:o + 2 * B_WIDTH]),
           y[:, o + 2 * B_WIDTH:o + 3 * B_WIDTH])
    _mla_up(y[:, o + 3 * B_WIDTH:], qg_ref, kvg_ref, wq_ref, wkv_ref, cc_ref, sc_ref,
            qc_ref, kc_ref, vc_ref)
    for i, (t, o_ref) in enumerate(zip(qkv, (q1_ref, k1_ref, v1_ref))):
        o_ref[...] = t.astype(BF16)
        slab_s[2 * i] = t[:, :LANES]
        slab_s[2 * i + 1] = t[:, LANES:]
    for dil, outs in ((4, (q4_ref, k4_ref, v4_ref)), (16, (q16_ref, k16_ref, v16_ref))):
        gs = tm // dil
        for i, o_ref in enumerate(outs):
            for r in range(dil):
                rows = pl.ds(r, gs, stride=dil)
                o_ref[r * gs:(r + 1) * gs, :] = jnp.concatenate(
                    [slab_s[2 * i, rows, :], slab_s[2 * i + 1, rows, :]], axis=1).astype(BF16)


def _mla_up(pc, qg_ref, kvg_ref, wq_ref, wkv_ref, cc_ref, sc_ref, q_ref, k_ref, v_ref):
    cq = _rms(pc[:, :C_Q_LORA], qg_ref[...])
    ckv = _rms(pc[:, C_Q_LORA:C_Q_LORA + C_KV_LORA], kvg_ref[...])
    kr = pc[:, C_Q_LORA + C_KV_LORA:]
    scale = (C_NOPE_DIM + C_ROPE_DIM) ** -0.5
    q = jnp.dot(cq.astype(BF16), wq_ref[...], preferred_element_type=F32) * scale
    kv = jnp.dot(ckv.astype(BF16), wkv_ref[...], preferred_element_type=F32)
    cc = cc_ref[...]
    sc = sc_ref[...]

    def rope(t):
        return t * cc + pltpu.roll(t, 64, 1) * sc

    krr = rope(kr).astype(BF16)
    for h in range(C_HEADS):
        o = h * C_QK_PAD
        q_ref[0, h, :, 0:C_NOPE_DIM] = q[:, o:o + C_NOPE_DIM].astype(BF16)
        q_ref[0, h, :, C_NOPE_DIM:] = rope(q[:, o + C_NOPE_DIM:o + C_QK_PAD]).astype(BF16)
        k_ref[0, h, :, 0:C_NOPE_DIM] = kv[:, h * C_NOPE_DIM:(h + 1) * C_NOPE_DIM].astype(BF16)
        k_ref[0, h, :, C_NOPE_DIM:] = krr
        vo = C_HEADS * C_NOPE_DIM + h * C_V_DIM
        v_ref[0, h, :, 0:C_V_DIM] = kv[:, vo:vo + C_V_DIM].astype(BF16)
        v_ref[0, h, :, C_V_DIM:] = jnp.ones((kv.shape[0], C_V_DIM), BF16)


def _in_proj(x2, g, w, cb, sb, qg, kvg, wq, wkv, cc, sc, bsz, seq, tm):
    n = x2.shape[0]
    nt = seq // tm
    row = lambda i: (i, 0)
    rb = lambda wd: pl.BlockSpec((tm, wd), row)
    res = lambda a: pl.BlockSpec(a.shape, lambda i: (0,) * a.ndim, pipeline_mode=pl.Buffered(1))
    hm = lambda wd: pl.BlockSpec((1, C_HEADS, tm, wd), lambda i: (i // nt, 0, i % nt, 0))
    bsd = jax.ShapeDtypeStruct((n, B_WIDTH), BF16)
    hsd = lambda wd: jax.ShapeDtypeStruct((bsz, C_HEADS, seq, wd), BF16)
    return pl.pallas_call(
        _in_proj_body,
        grid=(n // tm,),
        in_specs=[rb(D_MODEL), res(g), res(w), rb(LANES), rb(LANES),
                  res(qg), res(kvg), res(wq), res(wkv), rb(LANES), rb(LANES)],
        out_specs=[rb(A_PAD)] + [rb(B_WIDTH)] * 9 + [hm(C_QK_PAD), hm(C_QK_PAD), hm(2 * C_V_DIM)],
        out_shape=[jax.ShapeDtypeStruct((n, A_PAD), F32)] + [bsd] * 9
                  + [hsd(C_QK_PAD), hsd(C_QK_PAD), hsd(2 * C_V_DIM)],
        scratch_shapes=[pltpu.VMEM((6, tm, LANES), F32)],
        compiler_params=_cparams(("parallel",)),
        name="in_proj",
    )(x2, g, w, cb, sb, qg, kvg, wq, wkv, cc, sc)


def _rwkv_body(pa_ref, mu_ref, w0_ref, dup_ref, a0_ref, iup_ref, gup_ref, kk_ref, ka_ref,
               rk_ref, lnw_ref, lnb_ref, out_ref, prev_s, h_s, *, ts):
    T = A_CHUNK
    HT = A_HEADS * T
    W = A_WIDTH
    C = ts // T

    @pl.when(pl.program_id(1) == 0)
    def _():
        prev_s[...] = jnp.zeros_like(prev_s)
        h_s[...] = jnp.zeros_like(h_s)

    lane_w = lax.broadcasted_iota(jnp.int32, (W, W), 1)
    row_w = lax.broadcasted_iota(jnp.int32, (W, W), 0)
    ebd = jnp.where(lane_w // A_HEAD_DIM == row_w // A_HEAD_DIM, 1.0, 0.0).astype(F32)

    pa = pa_ref[...]
    rid = lax.broadcasted_iota(jnp.int32, (ts, 1), 0)
    shifted = jnp.where(rid == 0, prev_s[0:1, :], pltpu.roll(pa, 1, 0))
    prev_s[0:1, :] = pa[ts - 1:ts, :]
    pf = pa + (shifted - pa) * mu_ref[...]
    r = pf[:, 0:W]
    k = pf[:, W:2 * W]
    v = pf[:, 2 * W:3 * W]
    xwa = pf[:, 3 * W:3 * W + LANES]
    xg = pf[:, 3 * W + LANES:]
    dl = _dot_hi(jnp.tanh(xwa), dup_ref[...])
    z = -(w0_ref[...] + dl)
    softplus = jnp.maximum(z, 0.0) + jnp.log(1.0 + jnp.exp(-jnp.abs(z)))
    w_log = -softplus - 0.5
    lw = -jnp.exp(w_log)
    a = _sigmoid(a0_ref[...] + _dot(xwa, iup_ref[...]))
    g = _dot(_sigmoid(xg), gup_ref[...])
    kkv = k * kk_ref[...]
    ss = _dot(kkv * kkv, ebd)
    kn = kkv / jnp.maximum(jnp.sqrt(ss), 1e-12)
    k2 = k * (1.0 + (a - 1.0) * ka_ref[...])
    bon = _dot(r * k2 * rk_ref[...], ebd) * v
    b = kn * a

    tt = lax.broadcasted_iota(jnp.int32, (T, HT), 0)
    ss = lax.broadcasted_iota(jnp.int32, (T, HT), 1) % T
    strict = tt > ss
    lower = tt >= ss
    rr = lax.broadcasted_iota(jnp.int32, (HT, HT), 0)
    cc = lax.broadcasted_iota(jnp.int32, (HT, HT), 1)
    eye_w = rr == cc
    hmask = (rr // T) == (cc // A_HEAD_DIM)
    tr = lax.broadcasted_iota(jnp.int32, (T, T), 0)
    tc = lax.broadcasted_iota(jnp.int32, (T, T), 1)
    ltri = jnp.where(tr >= tc, 1.0, 0.0).astype(BF16)

    def stack(x):
        reps = x.shape[1] // W
        mask = hmask if reps == 1 else jnp.concatenate([hmask] * reps, axis=1)
        return jnp.where(mask, jnp.concatenate([x] * A_HEADS, axis=0), 0.0).astype(BF16)

    def chunks(x):
        return [x[c * T:(c + 1) * T, :] for c in range(C)]

    def each(fn, *lists):
        return [fn(*xs) for xs in zip(*lists)]

    rc, kc, vc, knc, bc, lwc = (chunks(t) for t in (r, k2, v, kn, b, lw))
    cum = each(lambda x: _cumsum_rows(x, ltri), lwc)
    e_in = each(jnp.exp, cum)
    rt = each(lambda x, e: x * e, rc, e_in)
    kt = each(lambda x, cu, l: x * jnp.exp(cu - l), knc, cum, lwc)
    e_out = each(lambda cu: jnp.exp(-cu), cum)
    bhs = each(lambda x, e: stack(x * e), bc, e_out)
    khs = each(lambda x, e: stack(x * e), kc, e_out)
    e_end = each(lambda cu: jnp.exp(cu[T - 1:T, :] - cu), cum)
    bg = each(lambda x, e: x * e, bc, e_end)
    kg = each(lambda x, e: x * e, kc, e_end)
    vs = each(stack, vc)

    a4 = each(lambda k_, r_, b_, kh_: _dot_nt(jnp.concatenate([k_, r_], axis=0),
                                              jnp.concatenate([b_, kh_], axis=0)), kt, rt, bhs, khs)
    a_ab = each(lambda t: jnp.where(strict, t[:T, :HT], 0.0), a4)
    a_akrk = each(lambda t: jnp.concatenate([jnp.where(strict, t[:T, HT:], 0.0),
                                             jnp.where(lower, t[T:, HT:], 0.0)], axis=0), a4)
    a_rb = each(lambda t: jnp.where(lower, t[T:, :HT], 0.0), a4)

    m = each(lambda t: jnp.where(tt == ss, 1.0, 0.0) - jnp.where((tt // 2 == ss // 2), t, 0.0), a_ab)
    s = 2
    while s < T:
        blk = (tt // (2 * s) == ss // (2 * s)) & ((tt // s) % 2 == 1) & ((ss // s) % 2 == 0)
        mc = each(lambda m_, t: _dot(m_, stack(jnp.where(blk, t, 0.0))), m, a_ab)
        m = each(lambda m_, t: m_ - _dot(t, stack(m_)), m, mc)
        s *= 2

    x1 = each(_dot, a_akrk, vs)
    pq = each(lambda m_, k_, x: _dot(m_, stack(jnp.concatenate([k_, x[:T]], axis=1))), m, kt, x1)
    x2 = each(lambda a_, t: _dot(a_, stack(t)), a_rb, pq)
    rp = each(lambda r_, x: r_ - x[:, :W], rt, x2)
    y0 = each(lambda x1_, x: x1_[T:] - x[:, W:], x1, x2)
    gd = each(_dot_tn, bg, pq)
    kv = each(_dot_tn, kg, vc)
    gmat = each(lambda e, t: jnp.where(eye_w, e[T - 1:T, :], 0.0) - jnp.where(hmask, t[:, :W], 0.0),
                e_in, gd)
    dmat = each(lambda kv_, t: jnp.where(hmask, kv_ - t[:, W:], 0.0), kv, gd)

    h = h_s[...]
    ys = []
    for c in range(C):
        t = _dot(jnp.concatenate([rp[c], gmat[c]], axis=0), h)
        ys.append(t[:T] + y0[c])
        h = t[T:] + dmat[c]
    h_s[...] = h
    y = jnp.concatenate(ys, axis=0)

    inv_n = 1.0 / A_HEAD_DIM
    mean = _dot(y, ebd) * inv_n
    yc = y - mean
    var = _dot(yc * yc, ebd) * inv_n
    yn = yc * lax.rsqrt(var + A_GN_EPS) * lnw_ref[...] + lnb_ref[...]
    out_ref[...] = (yn + bon) * g


def _cumsum_rows(x, ltri):
    hi, mid, lo = _split3(x)
    return (jnp.dot(ltri, hi, preferred_element_type=F32)
            + jnp.dot(ltri, mid, preferred_element_type=F32)
            + jnp.dot(ltri, lo, preferred_element_type=F32))


def _rwkv(pa, p, bsz, seq, ts):
    nt = seq // ts
    row = lambda b, i: (b * nt + i, 0)
    const = lambda b, i: (0, 0)
    vec = lambda w: pl.BlockSpec((1, w), const)
    return pl.pallas_call(
        functools.partial(_rwkv_body, ts=ts),
        grid=(bsz, nt),
        in_specs=[pl.BlockSpec((ts, A_PAD), row), vec(A_PAD), vec(A_WIDTH),
                  pl.BlockSpec((A_DECAY_LORA + A_ICLR_LORA, A_WIDTH), const), vec(A_WIDTH),
                  pl.BlockSpec((A_DECAY_LORA + A_ICLR_LORA, A_WIDTH), const),
                  pl.BlockSpec((A_WIDTH, A_WIDTH), const),
                  vec(A_WIDTH), vec(A_WIDTH), vec(A_WIDTH), vec(A_WIDTH), vec(A_WIDTH)],
        out_specs=pl.BlockSpec((ts, A_WIDTH), row),
        out_shape=jax.ShapeDtypeStruct((bsz * seq, A_WIDTH), F32),
        scratch_shapes=[pltpu.VMEM((8, A_PAD), F32), pltpu.VMEM((A_WIDTH, A_WIDTH), F32)],
        compiler_params=_cparams(("parallel", "arbitrary")),
        name="rwkv7",
    )(pa, p["mu"], p["w0"], p["dup"], p["a0"], p["iup"], p["gup"], p["k_k"], p["k_a"],
      p["r_k"], p["ln_w"], p["ln_b"])


def _dilated_body(q_ref, k_ref, v_ref, o_ref, l_ref, *, nb, gs, dil):
    Q = B_BLOCK
    HQ = B_HEADS * Q
    W = B_WIDTH
    rr = lax.broadcasted_iota(jnp.int32, (HQ, W), 0)
    cc = lax.broadcasted_iota(jnp.int32, (HQ, W), 1)
    hsel = jnp.where((rr // Q) == (cc // B_HEAD_DIM), 1.0, 0.0).astype(BF16)
    qi = rr % Q
    dist = qi + Q - cc
    band2 = (dist >= 0) & (dist <= Q)
    lane_h = lax.broadcasted_iota(jnp.int32, (Q, W), 1) // B_HEAD_DIM

    def where(n):
        if gs >= Q:
            per = gs // Q
            return n // per, slice((n % per) * Q, (n % per + 1) * Q)
        per = Q // gs
        return slice(n * per, (n + 1) * per), slice(None)

    def load(ref, r, n):
        g, rows = where(n)
        return ref[g, r, rows, :].reshape(Q, W)

    def store(ref, r, n, val):
        g, rows = where(n)
        ref[g, r, rows, :] = val.reshape(ref[g, r, rows, :].shape)

    def each(fn, *lists):
        return [fn(*xs) for xs in zip(*lists)]

    def scores(r, ns):
        qs = [jnp.concatenate([load(q_ref, r, n)] * B_HEADS, axis=0) * hsel for n in ns]
        kw = [jnp.concatenate([load(k_ref, r, max(n - 1, 0)), load(k_ref, r, n)], axis=0) for n in ns]
        s = each(lambda a, b: lax.dot_general(a, b, (((1,), (1,)), ((), ())),
                                              preferred_element_type=F32), qs, kw)
        return [jnp.where(band2 & (cc >= Q) if n == 0 else band2, t, NEG_BIG) for n, t in zip(ns, s)]

    def finish(r, ns, s):
        vw = [jnp.concatenate([load(v_ref, r, max(n - 1, 0)), load(v_ref, r, n)], axis=0) for n in ns]
        m = each(lambda t: jnp.max(t, axis=-1, keepdims=True), s)
        p = each(lambda t, m_: jnp.exp(t - m_), s, m)
        l = each(lambda t: jnp.sum(t, axis=-1, keepdims=True), p)
        o = each(lambda p_, v_, l_: jnp.dot(p_.astype(BF16), v_, preferred_element_type=F32) / l_,
                 p, vw, l)
        lse = each(lambda m_, l_: jnp.broadcast_to(m_ + jnp.log(l_), (HQ, W)), m, l)
        for n, o_, lse_ in zip(ns, o, lse):
            out = o_[0:Q]
            lout = lse_[0:Q]
            for h in range(1, B_HEADS):
                out = jnp.where(lane_h == h, o_[h * Q:(h + 1) * Q], out)
                lout = jnp.where(lane_h == h, lse_[h * Q:(h + 1) * Q], lout)
            store(o_ref, r, n, out.astype(o_ref.dtype))
            store(l_ref, r, n, lout)

    unroll = min(DIL_UNROLL, nb)
    groups = [(r, list(range(n0, n0 + unroll))) for r in range(dil) for n0 in range(0, nb, unroll)]
    s_next = scores(*groups[0])
    for gi, (r, ns) in enumerate(groups):
        s_cur = s_next
        if gi + 1 < len(groups):
            s_next = scores(*groups[gi + 1])
        finish(r, ns, s_cur)


def _dilated_branch(q, k, v, bsz, seq, dil, tm):
    gs = tm // dil
    nt = seq // tm
    view = lambda t: t.reshape(bsz, nt, dil, gs, B_WIDTH)
    spec = pl.BlockSpec((None, nt, dil, gs, B_WIDTH), lambda b: (b, 0, 0, 0, 0))
    o, l = pl.pallas_call(
        functools.partial(_dilated_body, nb=seq // dil // B_BLOCK, gs=gs, dil=dil),
        grid=(bsz,),
        in_specs=[spec, spec, spec],
        out_specs=[spec, spec],
        out_shape=[jax.ShapeDtypeStruct((bsz, nt, dil, gs, B_WIDTH), BF16),
                   jax.ShapeDtypeStruct((bsz, nt, dil, gs, B_WIDTH), F32)],
        compiler_params=_cparams(("parallel",)),
        name="dilated_d%d" % dil,
    )(view(q), view(k), view(v))
    return o.reshape(bsz * seq, B_WIDTH), l.reshape(bsz * seq, B_WIDTH)


def _flash_body(q_ref, k_ref, v_ref, o_ref, *, tq, nq):
    hq = tq // 2
    causal_top = (lax.broadcasted_iota(jnp.int32, (hq, hq), 1)
                  <= lax.broadcasted_iota(jnp.int32, (hq, hq), 0))
    causal_bot = (lax.broadcasted_iota(jnp.int32, (hq, tq), 1)
                  <= lax.broadcasted_iota(jnp.int32, (hq, tq), 0) + hq)
    blocks = [(i, j) for i in range(nq) for j in range(i + 1)]

    def qk(rows, keys):
        return lax.dot_general(q_ref[0, 0, rows, :], k_ref[0, 0, keys, :],
                               (((1,), (1,)), ((), ())), preferred_element_type=F32)

    def scores(i, j):
        if j < i:
            return (qk(slice(i * tq, (i + 1) * tq), slice(j * tq, (j + 1) * tq)),)
        return (qk(slice(i * tq, i * tq + hq), slice(j * tq, j * tq + hq)),
                qk(slice(i * tq + hq, (i + 1) * tq), slice(j * tq, (j + 1) * tq)))

    def update(m, l, acc, s, vb):
        m_new = jnp.maximum(m, jnp.max(s, axis=-1, keepdims=True))
        alpha = jnp.exp(m - m_new)
        p = jnp.exp(s - jnp.concatenate([m_new] * (s.shape[1] // LANES), axis=1))
        pv = jnp.dot(p.astype(BF16), vb, preferred_element_type=F32)
        return m_new, alpha * l + pv[:, C_V_DIM:], alpha * acc + pv[:, :C_V_DIM]

    s_next = scores(*blocks[0])
    m = l = acc = None
    for idx, (i, j) in enumerate(blocks):
        s = s_next
        if idx + 1 < len(blocks):
            s_next = scores(*blocks[idx + 1])
        if j == 0:
            m = jnp.full((tq, C_V_DIM), NEG_BIG, F32)
            l = jnp.zeros((tq, C_V_DIM), F32)
            acc = jnp.zeros((tq, C_V_DIM), F32)
        if j < i:
            m, l, acc = update(m, l, acc, s[0], v_ref[0, 0, j * tq:(j + 1) * tq, :])
        else:
            top = update(m[:hq], l[:hq], acc[:hq], jnp.where(causal_top, s[0], NEG_BIG),
                         v_ref[0, 0, j * tq:j * tq + hq, :])
            bot = update(m[hq:], l[hq:], acc[hq:], jnp.where(causal_bot, s[1], NEG_BIG),
                         v_ref[0, 0, j * tq:(j + 1) * tq, :])
            o_ref[0, i * tq:i * tq + hq, :] = top[2] / top[1]
            o_ref[0, i * tq + hq:(i + 1) * tq, :] = bot[2] / bot[1]


def _flash(q, k, v, bsz, seq, tq):
    nq = seq // tq
    hb = lambda w: pl.BlockSpec((1, 1, seq, w), lambda b, h: (b, h, 0, 0))
    return pl.pallas_call(
        functools.partial(_flash_body, tq=tq, nq=nq),
        grid=(bsz, C_HEADS),
        in_specs=[hb(C_QK_PAD), hb(C_QK_PAD), hb(2 * C_V_DIM)],
        out_specs=pl.BlockSpec((1, seq, C_V_DIM), lambda b, h: (b, 0, h)),
        out_shape=jax.ShapeDtypeStruct((bsz, seq, C_WIDTH), F32),
        compiler_params=_cparams(("parallel", "parallel")),
        name="mla_flash",
    )(q, k, v)


def _ffn_body(x_ref, ya_ref, o1_ref, o2_ref, o3_ref, l1_ref, l2_ref, l3_ref, yc_ref,
              wo_ref, g_ref, wg_ref, wu_ref, wd_ref, fg_ref, out_ref, h_s, slab_s, *, final):
    tm = x_ref.shape[0]

    def natural(dil, o_ref, l_ref):
        gs = tm // dil
        for r in range(dil):
            rows = pl.ds(r, gs, stride=dil)
            ov = o_ref[r * gs:(r + 1) * gs, :].astype(F32)
            lv = l_ref[r * gs:(r + 1) * gs, :]
            slab_s[0, rows, :] = ov[:, :LANES]
            slab_s[1, rows, :] = ov[:, LANES:]
            slab_s[2, rows, :] = lv[:, :LANES]
            slab_s[3, rows, :] = lv[:, LANES:]
        return (jnp.concatenate([slab_s[0], slab_s[1]], axis=1),
                jnp.concatenate([slab_s[2], slab_s[3]], axis=1))

    mix = (jnp.dot(ya_ref[...].astype(BF16), wo_ref[0:A_WIDTH, :], preferred_element_type=F32)
           + jnp.dot(yc_ref[...].astype(BF16), wo_ref[A_WIDTH + B_WIDTH:, :],
                     preferred_element_type=F32))
    o1, l1 = o1_ref[...].astype(F32), l1_ref[...]
    o2, l2 = natural(B_PATTERNS[1][1], o2_ref, l2_ref)
    o3, l3 = natural(B_PATTERNS[2][1], o3_ref, l3_ref)
    mx = jnp.maximum(jnp.maximum(l1, l2), l3)
    e1, e2, e3 = jnp.exp(l1 - mx), jnp.exp(l2 - mx), jnp.exp(l3 - mx)
    yb = (e1 * o1 + e2 * o2 + e3 * o3) / (e1 + e2 + e3)
    mix = mix + jnp.dot(yb.astype(BF16), wo_ref[A_WIDTH:A_WIDTH + B_WIDTH, :],
                        preferred_element_type=F32)
    xn = x_ref[...] + mix
    out_ref[...] = xn
    h_s[...] = _rms(xn, g_ref[...]).astype(BF16)

    for j in range(D_FF // FF_CHUNK):
        cols = slice(j * FF_CHUNK, (j + 1) * FF_CHUNK)
        h = h_s[...]
        gt = jnp.dot(h, wg_ref[:, cols], preferred_element_type=F32)
        up = jnp.dot(h, wu_ref[:, cols], preferred_element_type=F32)
        act = gt * _sigmoid(gt) * up
        out_ref[...] += jnp.dot(act.astype(BF16), wd_ref[cols, :], preferred_element_type=F32)
    if final:
        out_ref[...] = _rms(out_ref[...], fg_ref[...])


def _ffn(x2, ya, ob, lb, yc, wo, g, wg, wu, wd, fg, tm, final):
    n = x2.shape[0]
    row = lambda i: (i, 0)
    rb = lambda w: pl.BlockSpec((tm, w), row)
    res = lambda a: pl.BlockSpec(a.shape, lambda i: (0,) * a.ndim, pipeline_mode=pl.Buffered(1))
    return pl.pallas_call(
        functools.partial(_ffn_body, final=final),
        grid=(n // tm,),
        in_specs=[rb(D_MODEL), rb(A_WIDTH), rb(B_WIDTH), rb(B_WIDTH), rb(B_WIDTH),
                  rb(B_WIDTH), rb(B_WIDTH), rb(B_WIDTH), rb(C_WIDTH),
                  res(wo), res(g), res(wg), res(wu), res(wd), res(fg)],
        out_specs=pl.BlockSpec((tm, D_MODEL), row),
        out_shape=jax.ShapeDtypeStruct((n, D_MODEL), F32),
        scratch_shapes=[pltpu.VMEM((tm, D_MODEL), BF16), pltpu.VMEM((4, tm, LANES), F32)],
        compiler_params=_cparams(("parallel",)),
        name="out_proj_ffn",
    )(x2, ya, ob[0], ob[1], ob[2], lb[0], lb[1], lb[2], yc, wo, g, wg, wu, wd, fg)


def _pad_cols(w, width):
    return jnp.pad(w, ((0, 0), (0, width - w.shape[1])))


def _rope_layout_cols(w):
    z = jnp.zeros(w.shape[:-1] + (32,), w.dtype)
    return jnp.concatenate([w[..., :32], z, w[..., 32:], z], axis=-1)


def _layer_params(l, w_in, a_mu, a_w0, a_decay_up, a_a0, a_iclr_up, a_gate_up, a_k_k, a_k_a,
                  a_r_k, a_ln_w, a_ln_b, c_w_uq, c_w_ukv):
    wi = w_in[l]
    oa = A_PROJ
    ob = oa + 3 * B_WIDTH
    w_all = jnp.concatenate([
        _pad_cols(wi[:, :oa], A_PAD),
        wi[:, oa:ob],
        wi[:, ob:ob + C_Q_LORA + C_KV_LORA],
        _rope_layout_cols(wi[:, ob + C_Q_LORA + C_KV_LORA:]),
    ], axis=1).astype(BF16)
    z64 = jnp.zeros((64, A_WIDTH), F32)
    rw = lambda t: t.reshape(1, -1)
    wq = c_w_uq[l].reshape(C_Q_LORA, C_HEADS, C_NOPE_DIM + C_ROPE_DIM)
    wq = jnp.concatenate([wq[..., :C_NOPE_DIM], _rope_layout_cols(wq[..., C_NOPE_DIM:])], axis=-1)
    wkv = c_w_ukv[l].reshape(C_KV_LORA, C_HEADS, C_NOPE_DIM + C_V_DIM)
    wkv = jnp.concatenate([wkv[..., :C_NOPE_DIM].reshape(C_KV_LORA, -1),
                           wkv[..., C_NOPE_DIM:].reshape(C_KV_LORA, -1)], axis=1)
    return dict(
        w_all=w_all,
        mu=_pad_cols(rw(a_mu[l]), A_PAD), w0=rw(a_w0[l]),
        dup=jnp.concatenate([a_decay_up[l], z64], axis=0),
        a0=rw(a_a0[l]),
        iup=jnp.concatenate([z64, a_iclr_up[l]], axis=0),
        gup=jnp.pad(a_gate_up[l], ((0, A_WIDTH - A_GATE_LORA), (0, 0))),
        k_k=rw(a_k_k[l]), k_a=rw(a_k_a[l]), r_k=rw(a_r_k[l]),
        ln_w=rw(a_ln_w[l]), ln_b=rw(a_ln_b[l]),
        wq=wq.reshape(C_Q_LORA, C_HEADS * C_QK_PAD).astype(BF16),
        wkv=wkv.astype(BF16),
    )


def _rope_tables(positions):
    n = positions.size
    pos = positions.reshape(n, 1).astype(F32)

    def tables(dim):
        inv_freq = 1.0 / (ROPE_THETA ** (jnp.arange(0, dim, 2, dtype=F32) / dim))
        ang = pos * inv_freq
        return jnp.cos(ang), jnp.sin(ang)

    cb, sb = tables(B_ROT_DIM)
    one = jnp.ones((n, B_HEAD_DIM - B_ROT_DIM), F32)
    cb = jnp.tile(jnp.concatenate([cb, cb, one], axis=1), (1, LANES // B_HEAD_DIM))
    sb = jnp.tile(jnp.concatenate([-sb, sb, 0.0 * one], axis=1), (1, LANES // B_HEAD_DIM))
    cc, sc = tables(C_ROPE_DIM)
    z = jnp.zeros_like(cc)
    cc = jnp.concatenate([cc, z, cc, z], axis=1)
    sc = jnp.concatenate([-sc, z, sc, z], axis=1)
    return cb, sb, cc, sc


def kernel(x, positions, attn_norm_g, w_in, a_mu, a_w0, a_decay_up, a_a0, a_iclr_up, a_gate_up, a_k_k, a_k_a, a_r_k, a_ln_w, a_ln_b, c_q_norm_g, c_kv_norm_g, c_w_uq, c_w_ukv, w_out, ffn_norm_g, ffn_w_gate, ffn_w_up, ffn_w_down, final_norm_g):
    bsz, seq, _ = x.shape
    n = bsz * seq
    depth = w_in.shape[0]
    assert seq % max(w for w, _ in B_PATTERNS) == 0
    cb, sb, cc, sc = _rope_tables(positions)
    x2 = x.reshape(n, D_MODEL)
    tm = min(ROW_TILE, seq)
    for l in range(depth):
        p = _layer_params(l, w_in, a_mu, a_w0, a_decay_up, a_a0, a_iclr_up, a_gate_up, a_k_k,
                          a_k_a, a_r_k, a_ln_w, a_ln_b, c_w_uq, c_w_ukv)
        pa, *qkv = _in_proj(x2, attn_norm_g[l].reshape(1, -1), p["w_all"], cb, sb,
                            c_q_norm_g[l].reshape(1, -1), c_kv_norm_g[l].reshape(1, -1),
                            p["wq"], p["wkv"], cc, sc, bsz, seq, tm)
        qc, kc, vc = qkv[9:]
        ya = _rwkv(pa, p, bsz, seq, min(RWKV_TILE, seq))
        ob, lb = zip(*[_dilated_branch(*qkv[3 * i:3 * i + 3], bsz, seq, d, tm)
                       for i, (_, d) in enumerate(B_PATTERNS)])
        yc = _flash(qc, kc, vc, bsz, seq, min(FLASH_TILE, seq)).reshape(n, C_WIDTH)
        x2 = _ffn(x2, ya, ob, lb, yc, w_out[l].astype(BF16), ffn_norm_g[l].reshape(1, -1),
                  ffn_w_gate[l].astype(BF16), ffn_w_up[l].astype(BF16), ffn_w_down[l].astype(BF16),
                  final_norm_g.reshape(1, -1), tm, l == depth - 1)
    return x2.reshape(bsz, seq, D_MODEL)
```

```python
import functools

import jax
import jax.numpy as jnp
from jax import lax
from jax.experimental import pallas as pl
from jax.experimental.pallas import tpu as pltpu

F32 = jnp.float32
BF16 = jnp.bfloat16

D_MODEL = 1024
NORM_EPS = 1e-6
ROPE_THETA = 500000.0

A_HEADS = 4
A_HEAD_DIM = 64
A_WIDTH = 256
A_DECAY_LORA = 64
A_ICLR_LORA = 64
A_GATE_LORA = 160
A_GN_EPS = 64e-5
A_PROJ = 3 * A_WIDTH + A_DECAY_LORA + A_ICLR_LORA + A_GATE_LORA
A_PAD = 1152
A_CHUNK = 64
RWKV_TILE = 512

B_HEADS = 4
B_HEAD_DIM = 64
B_WIDTH = 256
B_ROT_DIM = 16
B_PATTERNS = ((128, 1), (512, 4), (2048, 16))
B_BLOCK = 128
DIL_UNROLL = 1

C_HEADS = 4
C_NOPE_DIM = 128
C_ROPE_DIM = 64
C_V_DIM = 128
C_Q_LORA = 256
C_KV_LORA = 128
C_WIDTH = 512
C_QK_PAD = 256
C_PAD = 512

MIX_WIDTH = 1024
P_PAD = A_PAD + 3 * B_WIDTH + C_PAD
D_FF = 2816
FF_CHUNK = 256

ROW_TILE = 512
FLASH_TILE = 512
LANES = 128
VMEM_LIMIT = 56 * 1024 * 1024
NEG_BIG = -1e30


def _cparams(sem):
    return pltpu.CompilerParams(dimension_semantics=sem, vmem_limit_bytes=VMEM_LIMIT)


def _rms(x, g):
    return x * lax.rsqrt(jnp.mean(x * x, axis=-1, keepdims=True) + NORM_EPS) * g


def _dot(a, b):
    return jnp.dot(a.astype(BF16), b.astype(BF16), preferred_element_type=F32)


def _dot_nt(a, b):
    return lax.dot_general(a.astype(BF16), b.astype(BF16), (((1,), (1,)), ((), ())),
                           preferred_element_type=F32)


def _dot_tn(a, b):
    return lax.dot_general(a.astype(BF16), b.astype(BF16), (((0,), (0,)), ((), ())),
                           preferred_element_type=F32)


def _split3(x):
    hi = x.astype(BF16)
    r1 = x - hi.astype(F32)
    mid = r1.astype(BF16)
    lo = (r1 - mid.astype(F32)).astype(BF16)
    return hi, mid, lo


def _dot_hi(a, b):
    ah = a.astype(BF16)
    al = (a - ah.astype(F32)).astype(BF16)
    bh = b.astype(BF16)
    bl = (b - bh.astype(F32)).astype(BF16)
    return (jnp.dot(ah, bh, preferred_element_type=F32)
            + jnp.dot(al, bh, preferred_element_type=F32)
            + jnp.dot(ah, bl, preferred_element_type=F32))


def _sigmoid(x):
    return 1.0 / (1.0 + jnp.exp(-x))


def _in_proj_body(x_ref, g_ref, w_ref, cb_ref, sb_ref, qg_ref, kvg_ref, wq_ref, wkv_ref, cc_ref, sc_ref,
                  pa_ref, q1_ref, k1_ref, v1_ref, q4_ref, k4_ref, v4_ref, q16_ref, k16_ref, v16_ref,
                  qc_ref, kc_ref, vc_ref, slab_s):
    tm = x_ref.shape[0]
    h = _rms(x_ref[...], g_ref[...])
    y = jnp.dot(h.astype(BF16), w_ref[...], preferred_element_type=F32)
    pa_ref[...] = y[:, :A_PAD]
    cb = jnp.concatenate([cb_ref[...]] * 2, axis=1)
    sb = jnp.concatenate([sb_ref[...]] * 2, axis=1)
    lane = lax.broadcasted_iota(jnp.int32, (1, B_WIDTH), 1)
    first = (lane % B_HEAD_DIM) < (B_ROT_DIM // 2)

    def rope(t):
        partner = jnp.where(first, pltpu.roll(t, B_WIDTH - B_ROT_DIM // 2, 1),
                            pltpu.roll(t, B_ROT_DIM // 2, 1))
        return t * cb + partner * sb

    o = A_PAD
    qkv = (rope(y[:, o:o + B_WIDTH]) * (B_HEAD_DIM ** -0.5),
           rope(y[:, o + B_WIDTH:o + 2 * B_WIDTH]),
           y[:, o + 2 * B_WIDTH:o + 3 * B_WIDTH])
    _mla_up(y[:, o + 3 * B_WIDTH:], qg_ref, kvg_ref, wq_ref, wkv_ref, cc_ref, sc_ref,
            qc_ref, kc_ref, vc_ref)
    for i, (t, o_ref) in enumerate(zip(qkv, (q1_ref, k1_ref, v1_ref))):
        o_ref[...] = t.astype(BF16)
        slab_s[2 * i] = t[:, :LANES]
        slab_s[2 * i + 1] = t[:, LANES:]
    for dil, outs in ((4, (q4_ref, k4_ref, v4_ref)), (16, (q16_ref, k16_ref, v16_ref))):
        gs = tm // dil
        for i, o_ref in enumerate(outs):
            for r in range(dil):
                rows = pl.ds(r, gs, stride=dil)
                o_ref[r * gs:(r + 1) * gs, :] = jnp.concatenate(
                    [slab_s[2 * i, rows, :], slab_s[2 * i + 1, rows, :]], axis=1).astype(BF16)


def _mla_up(pc, qg_ref, kvg_ref, wq_ref, wkv_ref, cc_ref, sc_ref, q_ref, k_ref, v_ref):
    cq = _rms(pc[:, :C_Q_LORA], qg_ref[...])
    ckv = _rms(pc[:, C_Q_LORA:C_Q_LORA + C_KV_LORA], kvg_ref[...])
    kr = pc[:, C_Q_LORA + C_KV_LORA:]
    scale = (C_NOPE_DIM + C_ROPE_DIM) ** -0.5
    q = jnp.dot(cq.astype(BF16), wq_ref[...], preferred_element_type=F32) * scale
    kv = jnp.dot(ckv.astype(BF16), wkv_ref[...], preferred_element_type=F32)
    cc = cc_ref[...]
    sc = sc_ref[...]

    def rope(t):
        return t * cc + pltpu.roll(t, 64, 1) * sc

    krr = rope(kr).astype(BF16)
    for h in range(C_HEADS):
        o = h * C_QK_PAD
        q_ref[0, h, :, 0:C_NOPE_DIM] = q[:, o:o + C_NOPE_DIM].astype(BF16)
        q_ref[0, h, :, C_NOPE_DIM:] = rope(q[:, o + C_NOPE_DIM:o + C_QK_PAD]).astype(BF16)
        k_ref[0, h, :, 0:C_NOPE_DIM] = kv[:, h * C_NOPE_DIM:(h + 1) * C_NOPE_DIM].astype(BF16)
        k_ref[0, h, :, C_NOPE_DIM:] = krr
        vo = C_HEADS * C_NOPE_DIM + h * C_V_DIM
        v_ref[0, h, :, 0:C_V_DIM] = kv[:, vo:vo + C_V_DIM].astype(BF16)
        v_ref[0, h, :, C_V_DIM:] = jnp.ones((kv.shape[0], C_V_DIM), BF16)


def _in_proj(x2, g, w, cb, sb, qg, kvg, wq, wkv, cc, sc, bsz, seq, tm):
    n = x2.shape[0]
    nt = seq // tm
    row = lambda i: (i, 0)
    rb = lambda wd: pl.BlockSpec((tm, wd), row)
    res = lambda a: pl.BlockSpec(a.shape, lambda i: (0,) * a.ndim, pipeline_mode=pl.Buffered(1))
    hm = lambda wd: pl.BlockSpec((1, C_HEADS, tm, wd), lambda i: (i // nt, 0, i % nt, 0))
    bsd = jax.ShapeDtypeStruct((n, B_WIDTH), BF16)
    hsd = lambda wd: jax.ShapeDtypeStruct((bsz, C_HEADS, seq, wd), BF16)
    return pl.pallas_call(
        _in_proj_body,
        grid=(n // tm,),
        in_specs=[rb(D_MODEL), res(g), res(w), rb(LANES), rb(LANES),
                  res(qg), res(kvg), res(wq), res(wkv), rb(LANES), rb(LANES)],
        out_specs=[rb(A_PAD)] + [rb(B_WIDTH)] * 9 + [hm(C_QK_PAD), hm(C_QK_PAD), hm(2 * C_V_DIM)],
        out_shape=[jax.ShapeDtypeStruct((n, A_PAD), F32)] + [bsd] * 9
                  + [hsd(C_QK_PAD), hsd(C_QK_PAD), hsd(2 * C_V_DIM)],
        scratch_shapes=[pltpu.VMEM((6, tm, LANES), F32)],
        compiler_params=_cparams(("parallel",)),
        name="in_proj",
    )(x2, g, w, cb, sb, qg, kvg, wq, wkv, cc, sc)


def _rwkv_body(pa_ref, mu_ref, w0_ref, dup_ref, a0_ref, iup_ref, gup_ref, kk_ref, ka_ref,
               rk_ref, lnw_ref, lnb_ref, out_ref, prev_s, h_s, *, ts):
    T = A_CHUNK
    HT = A_HEADS * T
    W = A_WIDTH
    C = ts // T

    @pl.when(pl.program_id(1) == 0)
    def _():
        prev_s[...] = jnp.zeros_like(prev_s)
        h_s[...] = jnp.zeros_like(h_s)

    lane_w = lax.broadcasted_iota(jnp.int32, (W, W), 1)
    row_w = lax.broadcasted_iota(jnp.int32, (W, W), 0)
    ebd = jnp.where(lane_w // A_HEAD_DIM == row_w // A_HEAD_DIM, 1.0, 0.0).astype(F32)

    pa = pa_ref[...]
    rid = lax.broadcasted_iota(jnp.int32, (ts, 1), 0)
    shifted = jnp.where(rid == 0, prev_s[0:1, :], pltpu.roll(pa, 1, 0))
    prev_s[0:1, :] = pa[ts - 1:ts, :]
    pf = pa + (shifted - pa) * mu_ref[...]
    r = pf[:, 0:W]
    k = pf[:, W:2 * W]
    v = pf[:, 2 * W:3 * W]
    xwa = pf[:, 3 * W:3 * W + LANES]
    xg = pf[:, 3 * W + LANES:]
    dl = _dot_hi(jnp.tanh(xwa), dup_ref[...])
    z = -(w0_ref[...] + dl)
    softplus = jnp.maximum(z, 0.0) + jnp.log(1.0 + jnp.exp(-jnp.abs(z)))
    w_log = -softplus - 0.5
    lw = -jnp.exp(w_log)
    a = _sigmoid(a0_ref[...] + _dot(xwa, iup_ref[...]))
    g = _dot(_sigmoid(xg), gup_ref[...])
    kkv = k * kk_ref[...]
    ss = _dot(kkv * kkv, ebd)
    kn = kkv / jnp.maximum(jnp.sqrt(ss), 1e-12)
    k2 = k * (1.0 + (a - 1.0) * ka_ref[...])
    bon = _dot(r * k2 * rk_ref[...], ebd) * v
    b = kn * a

    tt = lax.broadcasted_iota(jnp.int32, (T, HT), 0)
    ss = lax.broadcasted_iota(jnp.int32, (T, HT), 1) % T
    strict = tt > ss
    lower = tt >= ss
    rr = lax.broadcasted_iota(jnp.int32, (HT, HT), 0)
    cc = lax.broadcasted_iota(jnp.int32, (HT, HT), 1)
    eye_w = rr == cc
    hmask = (rr // T) == (cc // A_HEAD_DIM)
    tr = lax.broadcasted_iota(jnp.int32, (T, T), 0)
    tc = lax.broadcasted_iota(jnp.int32, (T, T), 1)
    ltri = jnp.where(tr >= tc, 1.0, 0.0).astype(BF16)

    def stack(x):
        reps = x.shape[1] // W
        mask = hmask if reps == 1 else jnp.concatenate([hmask] * reps, axis=1)
        return jnp.where(mask, jnp.concatenate([x] * A_HEADS, axis=0), 0.0).astype(BF16)

    def chunks(x):
        return [x[c * T:(c + 1) * T, :] for c in range(C)]

    def each(fn, *lists):
        return [fn(*xs) for xs in zip(*lists)]

    rc, kc, vc, knc, bc, lwc = (chunks(t) for t in (r, k2, v, kn, b, lw))
    cum = each(lambda x: _cumsum_rows(x, ltri), lwc)
    e_in = each(jnp.exp, cum)
    rt = each(lambda x, e: x * e, rc, e_in)
    kt = each(lambda x, cu, l: x * jnp.exp(cu - l), knc, cum, lwc)
    e_out = each(lambda cu: jnp.exp(-cu), cum)
    bhs = each(lambda x, e: stack(x * e), bc, e_out)
    khs = each(lambda x, e: stack(x * e), kc, e_out)
    e_end = each(lambda cu: jnp.exp(cu[T - 1:T, :] - cu), cum)
    bg = each(lambda x, e: x * e, bc, e_end)
    kg = each(lambda x, e: x * e, kc, e_end)
    vs = each(stack, vc)

    a4 = each(lambda k_, r_, b_, kh_: _dot_nt(jnp.concatenate([k_, r_], axis=0),
                                              jnp.concatenate([b_, kh_], axis=0)), kt, rt, bhs, khs)
    a_ab = each(lambda t: jnp.where(strict, t[:T, :HT], 0.0), a4)
    a_akrk = each(lambda t: jnp.concatenate([jnp.where(strict, t[:T, HT:], 0.0),
                                             jnp.where(lower, t[T:, HT:], 0.0)], axis=0), a4)
    a_rb = each(lambda t: jnp.where(lower, t[T:, :HT], 0.0), a4)

    m = each(lambda t: jnp.where(tt == ss, 1.0, 0.0) - jnp.where((tt // 2 == ss // 2), t, 0.0), a_ab)
    s = 2
    while s < T:
        blk = (tt // (2 * s) == ss // (2 * s)) & ((tt // s) % 2 == 1) & ((ss // s) % 2 == 0)
        mc = each(lambda m_, t: _dot(m_, stack(jnp.where(blk, t, 0.0))), m, a_ab)
        m = each(lambda m_, t: m_ - _dot(t, stack(m_)), m, mc)
        s *= 2

    x1 = each(_dot, a_akrk, vs)
    pq = each(lambda m_, k_, x: _dot(m_, stack(jnp.concatenate([k_, x[:T]], axis=1))), m, kt, x1)
    x2 = each(lambda a_, t: _dot(a_, stack(t)), a_rb, pq)
    rp = each(lambda r_, x: r_ - x[:, :W], rt, x2)
    y0 = each(lambda x1_, x: x1_[T:] - x[:, W:], x1, x2)
    gd = each(_dot_tn, bg, pq)
    kv = each(_dot_tn, kg, vc)
    gmat = each(lambda e, t: jnp.where(eye_w, e[T - 1:T, :], 0.0) - jnp.where(hmask, t[:, :W], 0.0),
                e_in, gd)
    dmat = each(lambda kv_, t: jnp.where(hmask, kv_ - t[:, W:], 0.0), kv, gd)

    h = h_s[...]
    ys = []
    for c in range(C):
        t = _dot(jnp.concatenate([rp[c], gmat[c]], axis=0), h)
        ys.append(t[:T] + y0[c])
        h = t[T:] + dmat[c]
    h_s[...] = h
    y = jnp.concatenate(ys, axis=0)

    inv_n = 1.0 / A_HEAD_DIM
    mean = _dot(y, ebd) * inv_n
    yc = y - mean
    var = _dot(yc * yc, ebd) * inv_n
    yn = yc * lax.rsqrt(var + A_GN_EPS) * lnw_ref[...] + lnb_ref[...]
    out_ref[...] = (yn + bon) * g


def _cumsum_rows(x, ltri):
    hi, mid, lo = _split3(x)
    return (jnp.dot(ltri, hi, preferred_element_type=F32)
            + jnp.dot(ltri, mid, preferred_element_type=F32)
            + jnp.dot(ltri, lo, preferred_element_type=F32))


def _rwkv(pa, p, bsz, seq, ts):
    nt = seq // ts
    row = lambda b, i: (b * nt + i, 0)
    const = lambda b, i: (0, 0)
    vec = lambda w: pl.BlockSpec((1, w), const)
    return pl.pallas_call(
        functools.partial(_rwkv_body, ts=ts),
        grid=(bsz, nt),
        in_specs=[pl.BlockSpec((ts, A_PAD), row), vec(A_PAD), vec(A_WIDTH),
                  pl.BlockSpec((A_DECAY_LORA + A_ICLR_LORA, A_WIDTH), const), vec(A_WIDTH),
                  pl.BlockSpec((A_DECAY_LORA + A_ICLR_LORA, A_WIDTH), const),
                  pl.BlockSpec((A_WIDTH, A_WIDTH), const),
                  vec(A_WIDTH), vec(A_WIDTH), vec(A_WIDTH), vec(A_WIDTH), vec(A_WIDTH)],
        out_specs=pl.BlockSpec((ts, A_WIDTH), row),
        out_shape=jax.ShapeDtypeStruct((bsz * seq, A_WIDTH), F32),
        scratch_shapes=[pltpu.VMEM((8, A_PAD), F32), pltpu.VMEM((A_WIDTH, A_WIDTH), F32)],
        compiler_params=_cparams(("parallel", "arbitrary")),
        name="rwkv7",
    )(pa, p["mu"], p["w0"], p["dup"], p["a0"], p["iup"], p["gup"], p["k_k"], p["k_a"],
      p["r_k"], p["ln_w"], p["ln_b"])


def _dilated_body(q_ref, k_ref, v_ref, o_ref, l_ref, *, nb, gs, dil):
    Q = B_BLOCK
    HQ = B_HEADS * Q
    W = B_WIDTH
    rr = lax.broadcasted_iota(jnp.int32, (HQ, W), 0)
    cc = lax.broadcasted_iota(jnp.int32, (HQ, W), 1)
    hsel = jnp.where((rr // Q) == (cc // B_HEAD_DIM), 1.0, 0.0).astype(BF16)
    qi = rr % Q
    dist = qi + Q - cc
    band2 = (dist >= 0) & (dist <= Q)
    lane_h = lax.broadcasted_iota(jnp.int32, (Q, W), 1) // B_HEAD_DIM

    def where(n):
        if gs >= Q:
            per = gs // Q
            return n // per, slice((n % per) * Q, (n % per + 1) * Q)
        per = Q // gs
        return slice(n * per, (n + 1) * per), slice(None)

    def load(ref, r, n):
        g, rows = where(n)
        return ref[g, r, rows, :].reshape(Q, W)

    def store(ref, r, n, val):
        g, rows = where(n)
        ref[g, r, rows, :] = val.reshape(ref[g, r, rows, :].shape)

    def each(fn, *lists):
        return [fn(*xs) for xs in zip(*lists)]

    def scores(r, ns):
        qs = [jnp.concatenate([load(q_ref, r, n)] * B_HEADS, axis=0) * hsel for n in ns]
        kw = [jnp.concatenate([load(k_ref, r, max(n - 1, 0)), load(k_ref, r, n)], axis=0) for n in ns]
        s = each(lambda a, b: lax.dot_general(a, b, (((1,), (1,)), ((), ())),
                                              preferred_element_type=F32), qs, kw)
        return [jnp.where(band2 & (cc >= Q) if n == 0 else band2, t, NEG_BIG) for n, t in zip(ns, s)]

    def finish(r, ns, s):
        vw = [jnp.concatenate([load(v_ref, r, max(n - 1, 0)), load(v_ref, r, n)], axis=0) for n in ns]
        m = each(lambda t: jnp.max(t, axis=-1, keepdims=True), s)
        p = each(lambda t, m_: jnp.exp(t - m_), s, m)
        l = each(lambda t: jnp.sum(t, axis=-1, keepdims=True), p)
        o = each(lambda p_, v_, l_: jnp.dot(p_.astype(BF16), v_, preferred_element_type=F32) / l_,
                 p, vw, l)
        lse = each(lambda m_, l_: jnp.broadcast_to(m_ + jnp.log(l_), (HQ, W)), m, l)
        for n, o_, lse_ in zip(ns, o, lse):
            out = o_[0:Q]
            lout = lse_[0:Q]
            for h in range(1, B_HEADS):
                out = jnp.where(lane_h == h, o_[h * Q:(h + 1) * Q], out)
                lout = jnp.where(lane_h == h, lse_[h * Q:(h + 1) * Q], lout)
            store(o_ref, r, n, out.astype(o_ref.dtype))
            store(l_ref, r, n, lout)

    unroll = min(DIL_UNROLL, nb)
    groups = [(r, list(range(n0, n0 + unroll))) for r in range(dil) for n0 in range(0, nb, unroll)]
    s_next = scores(*groups[0])
    for gi, (r, ns) in enumerate(groups):
        s_cur = s_next
        if gi + 1 < len(groups):
            s_next = scores(*groups[gi + 1])
        finish(r, ns, s_cur)


def _dilated_branch(q, k, v, bsz, seq, dil, tm):
    gs = tm // dil
    nt = seq // tm
    view = lambda t: t.reshape(bsz, nt, dil, gs, B_WIDTH)
    spec = pl.BlockSpec((None, nt, dil, gs, B_WIDTH), lambda b: (b, 0, 0, 0, 0))
    o, l = pl.pallas_call(
        functools.partial(_dilated_body, nb=seq // dil // B_BLOCK, gs=gs, dil=dil),
        grid=(bsz,),
        in_specs=[spec, spec, spec],
        out_specs=[spec, spec],
        out_shape=[jax.ShapeDtypeStruct((bsz, nt, dil, gs, B_WIDTH), BF16),
                   jax.ShapeDtypeStruct((bsz, nt, dil, gs, B_WIDTH), F32)],
        compiler_params=_cparams(("parallel",)),
        name="dilated_d%d" % dil,
    )(view(q), view(k), view(v))
    return o.reshape(bsz * seq, B_WIDTH), l.reshape(bsz * seq, B_WIDTH)


def _flash_body(q_ref, k_ref, v_ref, o_ref, *, tq, nq):
    hq = tq // 2
    causal_top = (lax.broadcasted_iota(jnp.int32, (hq, hq), 1)
                  <= lax.broadcasted_iota(jnp.int32, (hq, hq), 0))
    causal_bot = (lax.broadcasted_iota(jnp.int32, (hq, tq), 1)
                  <= lax.broadcasted_iota(jnp.int32, (hq, tq), 0) + hq)
    blocks = [(i, j) for i in range(nq) for j in range(i + 1)]

    def qk(rows, keys):
        return lax.dot_general(q_ref[0, 0, rows, :], k_ref[0, 0, keys, :],
                               (((1,), (1,)), ((), ())), preferred_element_type=F32)

    def scores(i, j):
        if j < i:
            return (qk(slice(i * tq, (i + 1) * tq), slice(j * tq, (j + 1) * tq)),)
        return (qk(slice(i * tq, i * tq + hq), slice(j * tq, j * tq + hq)),
                qk(slice(i * tq + hq, (i + 1) * tq), slice(j * tq, (j + 1) * tq)))

    def update(m, l, acc, s, vb):
        m_new = jnp.maximum(m, jnp.max(s, axis=-1, keepdims=True))
        alpha = jnp.exp(m - m_new)
        p = jnp.exp(s - jnp.concatenate([m_new] * (s.shape[1] // LANES), axis=1))
        pv = jnp.dot(p.astype(BF16), vb, preferred_element_type=F32)
        return m_new, alpha * l + pv[:, C_V_DIM:], alpha * acc + pv[:, :C_V_DIM]

    s_next = scores(*blocks[0])
    m = l = acc = None
    for idx, (i, j) in enumerate(blocks):
        s = s_next
        if idx + 1 < len(blocks):
            s_next = scores(*blocks[idx + 1])
        if j == 0:
            m = jnp.full((tq, C_V_DIM), NEG_BIG, F32)
            l = jnp.zeros((tq, C_V_DIM), F32)
            acc = jnp.zeros((tq, C_V_DIM), F32)
        if j < i:
            m, l, acc = update(m, l, acc, s[0], v_ref[0, 0, j * tq:(j + 1) * tq, :])
        else:
            top = update(m[:hq], l[:hq], acc[:hq], jnp.where(causal_top, s[0], NEG_BIG),
                         v_ref[0, 0, j * tq:j * tq + hq, :])
            bot = update(m[hq:], l[hq:], acc[hq:], jnp.where(causal_bot, s[1], NEG_BIG),
                         v_ref[0, 0, j * tq:(j + 1) * tq, :])
            o_ref[0, i * tq:i * tq + hq, :] = top[2] / top[1]
            o_ref[0, i * tq + hq:(i + 1) * tq, :] = bot[2] / bot[1]


def _flash(q, k, v, bsz, seq, tq):
    nq = seq // tq
    hb = lambda w: pl.BlockSpec((1, 1, seq, w), lambda b, h: (b, h, 0, 0))
    return pl.pallas_call(
        functools.partial(_flash_body, tq=tq, nq=nq),
        grid=(bsz, C_HEADS),
        in_specs=[hb(C_QK_PAD), hb(C_QK_PAD), hb(2 * C_V_DIM)],
        out_specs=pl.BlockSpec((1, seq, C_V_DIM), lambda b, h: (b, 0, h)),
        out_shape=jax.ShapeDtypeStruct((bsz, seq, C_WIDTH), F32),
        compiler_params=_cparams(("parallel", "parallel")),
        name="mla_flash",
    )(q, k, v)


def _ffn_body(x_ref, ya_ref, o1_ref, o2_ref, o3_ref, l1_ref, l2_ref, l3_ref, yc_ref,
              wo_ref, g_ref, wg_ref, wu_ref, wd_ref, fg_ref, out_ref, h_s, slab_s, *, final):
    tm = x_ref.shape[0]

    def natural(dil, o_ref, l_ref):
        gs = tm // dil
        for r in range(dil):
            rows = pl.ds(r, gs, stride=dil)
            ov = o_ref[r * gs:(r + 1) * gs, :].astype(F32)
            lv = l_ref[r * gs:(r + 1) * gs, :]
            slab_s[0, rows, :] = ov[:, :LANES]
            slab_s[1, rows, :] = ov[:, LANES:]
            slab_s[2, rows, :] = lv[:, :LANES]
            slab_s[3, rows, :] = lv[:, LANES:]
        return (jnp.concatenate([slab_s[0], slab_s[1]], axis=1),
                jnp.concatenate([slab_s[2], slab_s[3]], axis=1))

    mix = (jnp.dot(ya_ref[...].astype(BF16), wo_ref[0:A_WIDTH, :], preferred_element_type=F32)
           + jnp.dot(yc_ref[...].astype(BF16), wo_ref[A_WIDTH + B_WIDTH:, :],
                     preferred_element_type=F32))
    o1, l1 = o1_ref[...].astype(F32), l1_ref[...]
    o2, l2 = natural(B_PATTERNS[1][1], o2_ref, l2_ref)
    o3, l3 = natural(B_PATTERNS[2][1], o3_ref, l3_ref)
    mx = jnp.maximum(jnp.maximum(l1, l2), l3)
    e1, e2, e3 = jnp.exp(l1 - mx), jnp.exp(l2 - mx), jnp.exp(l3 - mx)
    yb = (e1 * o1 + e2 * o2 + e3 * o3) / (e1 + e2 + e3)
    mix = mix + jnp.dot(yb.astype(BF16), wo_ref[A_WIDTH:A_WIDTH + B_WIDTH, :],
                        preferred_element_type=F32)
    xn = x_ref[...] + mix
    out_ref[...] = xn
    h_s[...] = _rms(xn, g_ref[...]).astype(BF16)

    for j in range(D_FF // FF_CHUNK):
        cols = slice(j * FF_CHUNK, (j + 1) * FF_CHUNK)
        h = h_s[...]
        gt = jnp.dot(h, wg_ref[:, cols], preferred_element_type=F32)
        up = jnp.dot(h, wu_ref[:, cols], preferred_element_type=F32)
        act = gt * _sigmoid(gt) * up
        out_ref[...] += jnp.dot(act.astype(BF16), wd_ref[cols, :], preferred_element_type=F32)
    if final:
        out_ref[...] = _rms(out_ref[...], fg_ref[...])


def _ffn(x2, ya, ob, lb, yc, wo, g, wg, wu, wd, fg, tm, final):
    n = x2.shape[0]
    row = lambda i: (i, 0)
    rb = lambda w: pl.BlockSpec((tm, w), row)
    res = lambda a: pl.BlockSpec(a.shape, lambda i: (0,) * a.ndim, pipeline_mode=pl.Buffered(1))
    return pl.pallas_call(
        functools.partial(_ffn_body, final=final),
        grid=(n // tm,),
        in_specs=[rb(D_MODEL), rb(A_WIDTH), rb(B_WIDTH), rb(B_WIDTH), rb(B_WIDTH),
                  rb(B_WIDTH), rb(B_WIDTH), rb(B_WIDTH), rb(C_WIDTH),
                  res(wo), res(g), res(wg), res(wu), res(wd), res(fg)],
        out_specs=pl.BlockSpec((tm, D_MODEL), row),
        out_shape=jax.ShapeDtypeStruct((n, D_MODEL), F32),
        scratch_shapes=[pltpu.VMEM((tm, D_MODEL), BF16), pltpu.VMEM((4, tm, LANES), F32)],
        compiler_params=_cparams(("parallel",)),
        name="out_proj_ffn",
    )(x2, ya, ob[0], ob[1], ob[2], lb[0], lb[1], lb[2], yc, wo, g, wg, wu, wd, fg)


def _pad_cols(w, width):
    return jnp.pad(w, ((0, 0), (0, width - w.shape[1])))


def _rope_layout_cols(w):
    z = jnp.zeros(w.shape[:-1] + (32,), w.dtype)
    return jnp.concatenate([w[..., :32], z, w[..., 32:], z], axis=-1)


def _layer_params(l, w_in, a_mu, a_w0, a_decay_up, a_a0, a_iclr_up, a_gate_up, a_k_k, a_k_a,
                  a_r_k, a_ln_w, a_ln_b, c_w_uq, c_w_ukv):
    wi = w_in[l]
    oa = A_PROJ
    ob = oa + 3 * B_WIDTH
    w_all = jnp.concatenate([
        _pad_cols(wi[:, :oa], A_PAD),
        wi[:, oa:ob],
        wi[:, ob:ob + C_Q_LORA + C_KV_LORA],
        _rope_layout_cols(wi[:, ob + C_Q_LORA + C_KV_LORA:]),
    ], axis=1).astype(BF16)
    z64 = jnp.zeros((64, A_WIDTH), F32)
    rw = lambda t: t.reshape(1, -1)
    wq = c_w_uq[l].reshape(C_Q_LORA, C_HEADS, C_NOPE_DIM + C_ROPE_DIM)
    wq = jnp.concatenate([wq[..., :C_NOPE_DIM], _rope_layout_cols(wq[..., C_NOPE_DIM:])], axis=-1)
    wkv = c_w_ukv[l].reshape(C_KV_LORA, C_HEADS, C_NOPE_DIM + C_V_DIM)
    wkv = jnp.concatenate([wkv[..., :C_NOPE_DIM].reshape(C_KV_LORA, -1),
                           wkv[..., C_NOPE_DIM:].reshape(C_KV_LORA, -1)], axis=1)
    return dict(
        w_all=w_all,
        mu=_pad_cols(rw(a_mu[l]), A_PAD), w0=rw(a_w0[l]),
        dup=jnp.concatenate([a_decay_up[l], z64], axis=0),
        a0=rw(a_a0[l]),
        iup=jnp.concatenate([z64, a_iclr_up[l]], axis=0),
        gup=jnp.pad(a_gate_up[l], ((0, A_WIDTH - A_GATE_LORA), (0, 0))),
        k_k=rw(a_k_k[l]), k_a=rw(a_k_a[l]), r_k=rw(a_r_k[l]),
        ln_w=rw(a_ln_w[l]), ln_b=rw(a_ln_b[l]),
        wq=wq.reshape(C_Q_LORA, C_HEADS * C_QK_PAD).astype(BF16),
        wkv=wkv.astype(BF16),
    )


def _rope_body(pos_ref, freq_ref, cb_ref, sb_ref, cc_ref, sc_ref):
    ang = pos_ref[...] * freq_ref[...]
    c = jnp.cos(ang)
    s = jnp.sin(ang)
    lane = lax.broadcasted_iota(jnp.int32, (1, LANES), 1)
    nc, nb = C_ROPE_DIM // 2, B_ROT_DIM // 2
    is_c = lane < nc
    is_b = (lane >= nc) & (lane < nc + nb)
    c0 = jnp.where(is_c, c, 0.0)
    s0 = jnp.where(is_c, s, 0.0)
    cc_ref[...] = c0 + pltpu.roll(c0, LANES // 2, 1)
    sc_ref[...] = pltpu.roll(s0, LANES // 2, 1) - s0
    cb0 = pltpu.roll(jnp.where(is_b, c, 0.0), LANES - nc, 1)
    sb0 = pltpu.roll(jnp.where(is_b, s, 0.0), LANES - nc, 1)
    cb1 = cb0 + pltpu.roll(cb0, nb, 1)
    sb1 = pltpu.roll(sb0, nb, 1) - sb0
    rot = (lane % B_HEAD_DIM) < B_ROT_DIM
    cb_ref[...] = jnp.where(rot, cb1 + pltpu.roll(cb1, B_HEAD_DIM, 1), 1.0)
    sb_ref[...] = sb1 + pltpu.roll(sb1, B_HEAD_DIM, 1)


def _rope_tables(positions, tm):
    n = positions.size
    pos = positions.reshape(n, 1).astype(F32)
    inv_freq = lambda dim: 1.0 / (ROPE_THETA ** (jnp.arange(0, dim, 2, dtype=F32) / dim))
    freq = jnp.concatenate([inv_freq(C_ROPE_DIM), inv_freq(B_ROT_DIM),
                            jnp.zeros((LANES - C_ROPE_DIM // 2 - B_ROT_DIM // 2,), F32)]).reshape(1, LANES)
    tab = pl.BlockSpec((tm, LANES), lambda i: (i, 0))
    return pl.pallas_call(
        _rope_body,
        grid=(n // tm,),
        in_specs=[pl.BlockSpec((tm, 1), lambda i: (i, 0)), pl.BlockSpec((1, LANES), lambda i: (0, 0))],
        out_specs=[tab] * 4,
        out_shape=[jax.ShapeDtypeStruct((n, LANES), F32)] * 4,
        compiler_params=_cparams(("parallel",)),
        name="rope_tables",
    )(pos, freq)


def kernel(x, positions, attn_norm_g, w_in, a_mu, a_w0, a_decay_up, a_a0, a_iclr_up, a_gate_up, a_k_k, a_k_a, a_r_k, a_ln_w, a_ln_b, c_q_norm_g, c_kv_norm_g, c_w_uq, c_w_ukv, w_out, ffn_norm_g, ffn_w_gate, ffn_w_up, ffn_w_down, final_norm_g):
    bsz, seq, _ = x.shape
    n = bsz * seq
    depth = w_in.shape[0]
    assert seq % max(w for w, _ in B_PATTERNS) == 0
    x2 = x.reshape(n, D_MODEL)
    tm = min(ROW_TILE, seq)
    cb, sb, cc, sc = _rope_tables(positions, 4 * tm)
    for l in range(depth):
        p = _layer_params(l, w_in, a_mu, a_w0, a_decay_up, a_a0, a_iclr_up, a_gate_up, a_k_k,
                          a_k_a, a_r_k, a_ln_w, a_ln_b, c_w_uq, c_w_ukv)
        pa, *qkv = _in_proj(x2, attn_norm_g[l].reshape(1, -1), p["w_all"], cb, sb,
                            c_q_norm_g[l].reshape(1, -1), c_kv_norm_g[l].reshape(1, -1),
                            p["wq"], p["wkv"], cc, sc, bsz, seq, tm)
        qc, kc, vc = qkv[9:]
        ya = _rwkv(pa, p, bsz, seq, min(RWKV_TILE, seq))
        ob, lb = zip(*[_dilated_branch(*qkv[3 * i:3 * i + 3], bsz, seq, d, tm)
                       for i, (_, d) in enumerate(B_PATTERNS)])
        yc = _flash(qc, kc, vc, bsz, seq, min(FLASH_TILE, seq)).reshape(n, C_WIDTH)
        x2 = _ffn(x2, ya, ob, lb, yc, w_out[l].astype(BF16), ffn_norm_g[l].reshape(1, -1),
                  ffn_w_gate[l].astype(BF16), ffn_w_up[l].astype(BF16), ffn_w_down[l].astype(BF16),
                  final_norm_g.reshape(1, -1), tm, l == depth - 1)
    return x2.reshape(bsz, seq, D_MODEL)
```

```python
import functools

import jax
import jax.numpy as jnp
from jax import lax
from jax.experimental import pallas as pl
from jax.experimental.pallas import tpu as pltpu

F32 = jnp.float32
BF16 = jnp.bfloat16

D_MODEL = 1024
NORM_EPS = 1e-6
ROPE_THETA = 500000.0

A_HEADS = 4
A_HEAD_DIM = 64
A_WIDTH = 256
A_DECAY_LORA = 64
A_ICLR_LORA = 64
A_GATE_LORA = 160
A_GN_EPS = 64e-5
A_PROJ = 3 * A_WIDTH + A_DECAY_LORA + A_ICLR_LORA + A_GATE_LORA
A_PAD = 1152
A_CHUNK = 64
RWKV_TILE = 512

B_HEADS = 4
B_HEAD_DIM = 64
B_WIDTH = 256
B_ROT_DIM = 16
B_PATTERNS = ((128, 1), (512, 4), (2048, 16))
B_BLOCK = 128
DIL_UNROLL = 1

C_HEADS = 4
C_NOPE_DIM = 128
C_ROPE_DIM = 64
C_V_DIM = 128
C_Q_LORA = 256
C_KV_LORA = 128
C_WIDTH = 512
C_QK_PAD = 256
C_PAD = 512

MIX_WIDTH = 1024
P_PAD = A_PAD + 3 * B_WIDTH + C_PAD
D_FF = 2816
FF_CHUNK = 256

ROW_TILE = 512
FLASH_TILE = 512
LANES = 128
VMEM_LIMIT = 56 * 1024 * 1024
NEG_BIG = -1e30


def _cparams(sem):
    return pltpu.CompilerParams(dimension_semantics=sem, vmem_limit_bytes=VMEM_LIMIT)


def _rms(x, g):
    return x * lax.rsqrt(jnp.mean(x * x, axis=-1, keepdims=True) + NORM_EPS) * g


def _dot(a, b):
    return jnp.dot(a.astype(BF16), b.astype(BF16), preferred_element_type=F32)


def _dot_nt(a, b):
    return lax.dot_general(a.astype(BF16), b.astype(BF16), (((1,), (1,)), ((), ())),
                           preferred_element_type=F32)


def _dot_tn(a, b):
    return lax.dot_general(a.astype(BF16), b.astype(BF16), (((0,), (0,)), ((), ())),
                           preferred_element_type=F32)


def _split3(x):
    hi = x.astype(BF16)
    r1 = x - hi.astype(F32)
    mid = r1.astype(BF16)
    lo = (r1 - mid.astype(F32)).astype(BF16)
    return hi, mid, lo


def _dot_hi(a, b):
    ah = a.astype(BF16)
    al = (a - ah.astype(F32)).astype(BF16)
    bh = b.astype(BF16)
    bl = (b - bh.astype(F32)).astype(BF16)
    return (jnp.dot(ah, bh, preferred_element_type=F32)
            + jnp.dot(al, bh, preferred_element_type=F32)
            + jnp.dot(ah, bl, preferred_element_type=F32))


def _sigmoid(x):
    return 1.0 / (1.0 + jnp.exp(-x))


def _in_proj_body(x_ref, g_ref, w_ref, cb_ref, sb_ref, qg_ref, kvg_ref, wq_ref, wkv_ref, cc_ref, sc_ref,
                  pa_ref, q1_ref, k1_ref, v1_ref, q4_ref, k4_ref, v4_ref, q16_ref, k16_ref, v16_ref,
                  qc_ref, kc_ref, vc_ref, slab_s):
    tm = x_ref.shape[0]
    h = _rms(x_ref[...], g_ref[...])
    y = jnp.dot(h.astype(BF16), w_ref[...], preferred_element_type=F32)
    pa_ref[...] = y[:, :A_PAD]
    cb = jnp.concatenate([cb_ref[...]] * 2, axis=1)
    sb = jnp.concatenate([sb_ref[...]] * 2, axis=1)
    lane = lax.broadcasted_iota(jnp.int32, (1, B_WIDTH), 1)
    first = (lane % B_HEAD_DIM) < (B_ROT_DIM // 2)

    def rope(t):
        partner = jnp.where(first, pltpu.roll(t, B_WIDTH - B_ROT_DIM // 2, 1),
                            pltpu.roll(t, B_ROT_DIM // 2, 1))
        return t * cb + partner * sb

    o = A_PAD
    qkv = (rope(y[:, o:o + B_WIDTH]) * (B_HEAD_DIM ** -0.5),
           rope(y[:, o + B_WIDTH:o + 2 * B_WIDTH]),
           y[:, o + 2 * B_WIDTH:o + 3 * B_WIDTH])
    _mla_up(y[:, o + 3 * B_WIDTH:], qg_ref, kvg_ref, wq_ref, wkv_ref, cc_ref, sc_ref,
            qc_ref, kc_ref, vc_ref)
    for i, (t, o_ref) in enumerate(zip(qkv, (q1_ref, k1_ref, v1_ref))):
        o_ref[...] = t.astype(BF16)
        slab_s[2 * i] = t[:, :LANES]
        slab_s[2 * i + 1] = t[:, LANES:]
    for dil, outs in ((4, (q4_ref, k4_ref, v4_ref)), (16, (q16_ref, k16_ref, v16_ref))):
        gs = tm // dil
        for i, o_ref in enumerate(outs):
            for r in range(dil):
                rows = pl.ds(r, gs, stride=dil)
                o_ref[r * gs:(r + 1) * gs, :] = jnp.concatenate(
                    [slab_s[2 * i, rows, :], slab_s[2 * i + 1, rows, :]], axis=1).astype(BF16)


def _mla_up(pc, qg_ref, kvg_ref, wq_ref, wkv_ref, cc_ref, sc_ref, q_ref, k_ref, v_ref):
    cq = _rms(pc[:, :C_Q_LORA], qg_ref[...])
    ckv = _rms(pc[:, C_Q_LORA:C_Q_LORA + C_KV_LORA], kvg_ref[...])
    kr = pc[:, C_Q_LORA + C_KV_LORA:]
    scale = (C_NOPE_DIM + C_ROPE_DIM) ** -0.5
    q = jnp.dot(cq.astype(BF16), wq_ref[...], preferred_element_type=F32) * scale
    kv = jnp.dot(ckv.astype(BF16), wkv_ref[...], preferred_element_type=F32)
    cc = cc_ref[...]
    sc = sc_ref[...]

    def rope(t):
        return t * cc + pltpu.roll(t, 64, 1) * sc

    krr = rope(kr).astype(BF16)
    for h in range(C_HEADS):
        o = h * C_QK_PAD
        q_ref[0, h, :, 0:C_NOPE_DIM] = q[:, o:o + C_NOPE_DIM].astype(BF16)
        q_ref[0, h, :, C_NOPE_DIM:] = rope(q[:, o + C_NOPE_DIM:o + C_QK_PAD]).astype(BF16)
        k_ref[0, h, :, 0:C_NOPE_DIM] = kv[:, h * C_NOPE_DIM:(h + 1) * C_NOPE_DIM].astype(BF16)
        k_ref[0, h, :, C_NOPE_DIM:] = krr
        vo = C_HEADS * C_NOPE_DIM + h * C_V_DIM
        v_ref[0, h, :, 0:C_V_DIM] = kv[:, vo:vo + C_V_DIM].astype(BF16)
        v_ref[0, h, :, C_V_DIM:] = jnp.ones((kv.shape[0], C_V_DIM), BF16)


def _in_proj(x2, g, w, cb, sb, qg, kvg, wq, wkv, cc, sc, bsz, seq, tm):
    n = x2.shape[0]
    nt = seq // tm
    row = lambda i: (i, 0)
    rb = lambda wd: pl.BlockSpec((tm, wd), row)
    res = lambda a: pl.BlockSpec(a.shape, lambda i: (0,) * a.ndim, pipeline_mode=pl.Buffered(1))
    hm = lambda wd: pl.BlockSpec((1, C_HEADS, tm, wd), lambda i: (i // nt, 0, i % nt, 0))
    bsd = jax.ShapeDtypeStruct((n, B_WIDTH), BF16)
    hsd = lambda wd: jax.ShapeDtypeStruct((bsz, C_HEADS, seq, wd), BF16)
    return pl.pallas_call(
        _in_proj_body,
        grid=(n // tm,),
        in_specs=[rb(D_MODEL), res(g), res(w), rb(LANES), rb(LANES),
                  res(qg), res(kvg), res(wq), res(wkv), rb(LANES), rb(LANES)],
        out_specs=[rb(A_PAD)] + [rb(B_WIDTH)] * 9 + [hm(C_QK_PAD), hm(C_QK_PAD), hm(2 * C_V_DIM)],
        out_shape=[jax.ShapeDtypeStruct((n, A_PAD), F32)] + [bsd] * 9
                  + [hsd(C_QK_PAD), hsd(C_QK_PAD), hsd(2 * C_V_DIM)],
        scratch_shapes=[pltpu.VMEM((6, tm, LANES), F32)],
        compiler_params=_cparams(("parallel",)),
        name="in_proj",
    )(x2, g, w, cb, sb, qg, kvg, wq, wkv, cc, sc)


def _rwkv_body(pa_ref, mu_ref, w0_ref, dup_ref, a0_ref, iup_ref, gup_ref, kk_ref, ka_ref,
               rk_ref, lnw_ref, lnb_ref, out_ref, prev_s, h_s, *, ts):
    T = A_CHUNK
    HT = A_HEADS * T
    W = A_WIDTH
    C = ts // T

    @pl.when(pl.program_id(1) == 0)
    def _():
        prev_s[...] = jnp.zeros_like(prev_s)
        h_s[...] = jnp.zeros_like(h_s)

    lane_w = lax.broadcasted_iota(jnp.int32, (W, W), 1)
    row_w = lax.broadcasted_iota(jnp.int32, (W, W), 0)
    ebd = jnp.where(lane_w // A_HEAD_DIM == row_w // A_HEAD_DIM, 1.0, 0.0).astype(F32)

    pa = pa_ref[...]
    rid = lax.broadcasted_iota(jnp.int32, (ts, 1), 0)
    shifted = jnp.where(rid == 0, prev_s[0:1, :], pltpu.roll(pa, 1, 0))
    prev_s[0:1, :] = pa[ts - 1:ts, :]
    pf = pa + (shifted - pa) * mu_ref[...]
    r = pf[:, 0:W]
    k = pf[:, W:2 * W]
    v = pf[:, 2 * W:3 * W]
    xwa = pf[:, 3 * W:3 * W + LANES]
    xg = pf[:, 3 * W + LANES:]
    dl = _dot_hi(jnp.tanh(xwa), dup_ref[...])
    z = -(w0_ref[...] + dl)
    softplus = jnp.maximum(z, 0.0) + jnp.log(1.0 + jnp.exp(-jnp.abs(z)))
    w_log = -softplus - 0.5
    lw = -jnp.exp(w_log)
    a = _sigmoid(a0_ref[...] + _dot(xwa, iup_ref[...]))
    g = _dot(_sigmoid(xg), gup_ref[...])
    kkv = k * kk_ref[...]
    ss = _dot(kkv * kkv, ebd)
    kn = kkv / jnp.maximum(jnp.sqrt(ss), 1e-12)
    k2 = k * (1.0 + (a - 1.0) * ka_ref[...])
    bon = _dot(r * k2 * rk_ref[...], ebd) * v
    b = kn * a

    tt = lax.broadcasted_iota(jnp.int32, (T, HT), 0)
    ss = lax.broadcasted_iota(jnp.int32, (T, HT), 1) % T
    strict = tt > ss
    lower = tt >= ss
    rr = lax.broadcasted_iota(jnp.int32, (HT, HT), 0)
    cc = lax.broadcasted_iota(jnp.int32, (HT, HT), 1)
    eye_w = rr == cc
    hmask = (rr // T) == (cc // A_HEAD_DIM)
    tr = lax.broadcasted_iota(jnp.int32, (T, T), 0)
    tc = lax.broadcasted_iota(jnp.int32, (T, T), 1)
    ltri = jnp.where(tr >= tc, 1.0, 0.0).astype(BF16)

    def stack(x):
        reps = x.shape[1] // W
        mask = hmask if reps == 1 else jnp.concatenate([hmask] * reps, axis=1)
        return jnp.where(mask, jnp.concatenate([x] * A_HEADS, axis=0), 0.0).astype(BF16)

    def chunks(x):
        return [x[c * T:(c + 1) * T, :] for c in range(C)]

    def each(fn, *lists):
        return [fn(*xs) for xs in zip(*lists)]

    rc, kc, vc, knc, bc, lwc = (chunks(t) for t in (r, k2, v, kn, b, lw))
    cum = each(lambda x: _cumsum_rows(x, ltri), lwc)
    e_in = each(jnp.exp, cum)
    rt = each(lambda x, e: x * e, rc, e_in)
    kt = each(lambda x, cu, l: x * jnp.exp(cu - l), knc, cum, lwc)
    e_out = each(lambda cu: jnp.exp(-cu), cum)
    bhs = each(lambda x, e: stack(x * e), bc, e_out)
    khs = each(lambda x, e: stack(x * e), kc, e_out)
    e_end = each(lambda cu: jnp.exp(cu[T - 1:T, :] - cu), cum)
    bg = each(lambda x, e: x * e, bc, e_end)
    kg = each(lambda x, e: x * e, kc, e_end)
    vs = each(stack, vc)

    a4 = each(lambda k_, r_, b_, kh_: _dot_nt(jnp.concatenate([k_, r_], axis=0),
                                              jnp.concatenate([b_, kh_], axis=0)), kt, rt, bhs, khs)
    a_ab = each(lambda t: jnp.where(strict, t[:T, :HT], 0.0), a4)
    a_akrk = each(lambda t: jnp.concatenate([jnp.where(strict, t[:T, HT:], 0.0),
                                             jnp.where(lower, t[T:, HT:], 0.0)], axis=0), a4)
    a_rb = each(lambda t: jnp.where(lower, t[T:, :HT], 0.0), a4)

    m = each(lambda t: jnp.where(tt == ss, 1.0, 0.0) - jnp.where((tt // 2 == ss // 2), t, 0.0), a_ab)
    s = 2
    while s < T:
        blk = (tt // (2 * s) == ss // (2 * s)) & ((tt // s) % 2 == 1) & ((ss // s) % 2 == 0)
        mc = each(lambda m_, t: _dot(m_, stack(jnp.where(blk, t, 0.0))), m, a_ab)
        m = each(lambda m_, t: m_ - _dot(t, stack(m_)), m, mc)
        s *= 2

    x1 = each(_dot, a_akrk, vs)
    pq = each(lambda m_, k_, x: _dot(m_, stack(jnp.concatenate([k_, x[:T]], axis=1))), m, kt, x1)
    x2 = each(lambda a_, t: _dot(a_, stack(t)), a_rb, pq)
    rp = each(lambda r_, x: r_ - x[:, :W], rt, x2)
    y0 = each(lambda x1_, x: x1_[T:] - x[:, W:], x1, x2)
    gd = each(_dot_tn, bg, pq)
    kv = each(_dot_tn, kg, vc)
    gmat = each(lambda e, t: jnp.where(eye_w, e[T - 1:T, :], 0.0) - jnp.where(hmask, t[:, :W], 0.0),
                e_in, gd)
    dmat = each(lambda kv_, t: jnp.where(hmask, kv_ - t[:, W:], 0.0), kv, gd)

    h = h_s[...]
    ys = []
    for c in range(C):
        t = _dot(jnp.concatenate([rp[c], gmat[c]], axis=0), h)
        ys.append(t[:T] + y0[c])
        h = t[T:] + dmat[c]
    h_s[...] = h
    y = jnp.concatenate(ys, axis=0)

    inv_n = 1.0 / A_HEAD_DIM
    mean = _dot(y, ebd) * inv_n
    yc = y - mean
    var = _dot(yc * yc, ebd) * inv_n
    yn = yc * lax.rsqrt(var + A_GN_EPS) * lnw_ref[...] + lnb_ref[...]
    out_ref[...] = (yn + bon) * g


def _cumsum_rows(x, ltri):
    hi, mid, lo = _split3(x)
    return (jnp.dot(ltri, hi, preferred_element_type=F32)
            + jnp.dot(ltri, mid, preferred_element_type=F32)
            + jnp.dot(ltri, lo, preferred_element_type=F32))


def _rwkv(pa, p, bsz, seq, ts):
    nt = seq // ts
    row = lambda b, i: (b * nt + i, 0)
    const = lambda b, i: (0, 0)
    vec = lambda w: pl.BlockSpec((1, w), const)
    return pl.pallas_call(
        functools.partial(_rwkv_body, ts=ts),
        grid=(bsz, nt),
        in_specs=[pl.BlockSpec((ts, A_PAD), row), vec(A_PAD), vec(A_WIDTH),
                  pl.BlockSpec((A_DECAY_LORA + A_ICLR_LORA, A_WIDTH), const), vec(A_WIDTH),
                  pl.BlockSpec((A_DECAY_LORA + A_ICLR_LORA, A_WIDTH), const),
                  pl.BlockSpec((A_WIDTH, A_WIDTH), const),
                  vec(A_WIDTH), vec(A_WIDTH), vec(A_WIDTH), vec(A_WIDTH), vec(A_WIDTH)],
        out_specs=pl.BlockSpec((ts, A_WIDTH), row),
        out_shape=jax.ShapeDtypeStruct((bsz * seq, A_WIDTH), F32),
        scratch_shapes=[pltpu.VMEM((8, A_PAD), F32), pltpu.VMEM((A_WIDTH, A_WIDTH), F32)],
        compiler_params=_cparams(("parallel", "arbitrary")),
        name="rwkv7",
    )(pa, p["mu"], p["w0"], p["dup"], p["a0"], p["iup"], p["gup"], p["k_k"], p["k_a"],
      p["r_k"], p["ln_w"], p["ln_b"])


def _dilated_body(q_ref, k_ref, v_ref, o_ref, l_ref, *, nb, gs, dil):
    Q = B_BLOCK
    HQ = B_HEADS * Q
    W = B_WIDTH
    rr = lax.broadcasted_iota(jnp.int32, (HQ, W), 0)
    cc = lax.broadcasted_iota(jnp.int32, (HQ, W), 1)
    hsel = jnp.where((rr // Q) == (cc // B_HEAD_DIM), 1.0, 0.0).astype(BF16)
    qi = rr % Q
    dist = qi + Q - cc
    band2 = (dist >= 0) & (dist <= Q)
    lane_h = lax.broadcasted_iota(jnp.int32, (Q, W), 1) // B_HEAD_DIM

    def where(n):
        if gs >= Q:
            per = gs // Q
            return n // per, slice((n % per) * Q, (n % per + 1) * Q)
        per = Q // gs
        return slice(n * per, (n + 1) * per), slice(None)

    def load(ref, r, n):
        g, rows = where(n)
        return ref[g, r, rows, :].reshape(Q, W)

    def store(ref, r, n, val):
        g, rows = where(n)
        ref[g, r, rows, :] = val.reshape(ref[g, r, rows, :].shape)

    def each(fn, *lists):
        return [fn(*xs) for xs in zip(*lists)]

    def scores(r, ns):
        qs = [jnp.concatenate([load(q_ref, r, n)] * B_HEADS, axis=0) * hsel for n in ns]
        kw = [jnp.concatenate([load(k_ref, r, max(n - 1, 0)), load(k_ref, r, n)], axis=0) for n in ns]
        s = each(lambda a, b: lax.dot_general(a, b, (((1,), (1,)), ((), ())),
                                              preferred_element_type=F32), qs, kw)
        return [jnp.where(band2 & (cc >= Q) if n == 0 else band2, t, NEG_BIG) for n, t in zip(ns, s)]

    def finish(r, ns, s):
        vw = [jnp.concatenate([load(v_ref, r, max(n - 1, 0)), load(v_ref, r, n)], axis=0) for n in ns]
        m = each(lambda t: jnp.max(t, axis=-1, keepdims=True), s)
        p = each(lambda t, m_: jnp.exp(t - m_), s, m)
        l = each(lambda t: jnp.sum(t, axis=-1, keepdims=True), p)
        o = each(lambda p_, v_, l_: jnp.dot(p_.astype(BF16), v_, preferred_element_type=F32) / l_,
                 p, vw, l)
        lse = each(lambda m_, l_: jnp.broadcast_to(m_ + jnp.log(l_), (HQ, W)), m, l)
        for n, o_, lse_ in zip(ns, o, lse):
            out = o_[0:Q]
            lout = lse_[0:Q]
            for h in range(1, B_HEADS):
                out = jnp.where(lane_h == h, o_[h * Q:(h + 1) * Q], out)
                lout = jnp.where(lane_h == h, lse_[h * Q:(h + 1) * Q], lout)
            store(o_ref, r, n, out.astype(o_ref.dtype))
            store(l_ref, r, n, lout)

    unroll = min(DIL_UNROLL, nb)
    groups = [(r, list(range(n0, n0 + unroll))) for r in range(dil) for n0 in range(0, nb, unroll)]
    s_next = scores(*groups[0])
    for gi, (r, ns) in enumerate(groups):
        s_cur = s_next
        if gi + 1 < len(groups):
            s_next = scores(*groups[gi + 1])
        finish(r, ns, s_cur)


def _dilated_branch(q, k, v, bsz, seq, dil, tm):
    gs = tm // dil
    nt = seq // tm
    view = lambda t: t.reshape(bsz, nt, dil, gs, B_WIDTH)
    spec = pl.BlockSpec((None, nt, dil, gs, B_WIDTH), lambda b: (b, 0, 0, 0, 0))
    o, l = pl.pallas_call(
        functools.partial(_dilated_body, nb=seq // dil // B_BLOCK, gs=gs, dil=dil),
        grid=(bsz,),
        in_specs=[spec, spec, spec],
        out_specs=[spec, spec],
        out_shape=[jax.ShapeDtypeStruct((bsz, nt, dil, gs, B_WIDTH), BF16),
                   jax.ShapeDtypeStruct((bsz, nt, dil, gs, B_WIDTH), F32)],
        compiler_params=_cparams(("parallel",)),
        name="dilated_d%d" % dil,
    )(view(q), view(k), view(v))
    return o.reshape(bsz * seq, B_WIDTH), l.reshape(bsz * seq, B_WIDTH)


def _flash_body(q_ref, k_ref, v_ref, o_ref, *, tq, nq):
    hq = tq // 2
    causal_top = (lax.broadcasted_iota(jnp.int32, (hq, hq), 1)
                  <= lax.broadcasted_iota(jnp.int32, (hq, hq), 0))
    causal_bot = (lax.broadcasted_iota(jnp.int32, (hq, tq), 1)
                  <= lax.broadcasted_iota(jnp.int32, (hq, tq), 0) + hq)
    blocks = [(i, j) for i in range(nq) for j in range(i + 1)]

    def qk(rows, keys):
        return lax.dot_general(q_ref[0, 0, rows, :], k_ref[0, 0, keys, :],
                               (((1,), (1,)), ((), ())), preferred_element_type=F32)

    def scores(i, j):
        if j < i:
            return (qk(slice(i * tq, (i + 1) * tq), slice(j * tq, (j + 1) * tq)),)
        return (qk(slice(i * tq, i * tq + hq), slice(j * tq, j * tq + hq)),
                qk(slice(i * tq + hq, (i + 1) * tq), slice(j * tq, (j + 1) * tq)))

    def update(m, l, acc, s, vb):
        m_new = jnp.maximum(m, jnp.max(s, axis=-1, keepdims=True))
        alpha = jnp.exp(m - m_new)
        p = jnp.exp(s - jnp.concatenate([m_new] * (s.shape[1] // LANES), axis=1))
        pv = jnp.dot(p.astype(BF16), vb, preferred_element_type=F32)
        return m_new, alpha * l + pv[:, C_V_DIM:], alpha * acc + pv[:, :C_V_DIM]

    s_next = scores(*blocks[0])
    m = l = acc = None
    for idx, (i, j) in enumerate(blocks):
        s = s_next
        if idx + 1 < len(blocks):
            s_next = scores(*blocks[idx + 1])
        if j == 0:
            m = jnp.full((tq, C_V_DIM), NEG_BIG, F32)
            l = jnp.zeros((tq, C_V_DIM), F32)
            acc = jnp.zeros((tq, C_V_DIM), F32)
        if j < i:
            m, l, acc = update(m, l, acc, s[0], v_ref[0, 0, j * tq:(j + 1) * tq, :])
        else:
            top = update(m[:hq], l[:hq], acc[:hq], jnp.where(causal_top, s[0], NEG_BIG),
                         v_ref[0, 0, j * tq:j * tq + hq, :])
            bot = update(m[hq:], l[hq:], acc[hq:], jnp.where(causal_bot, s[1], NEG_BIG),
                         v_ref[0, 0, j * tq:(j + 1) * tq, :])
            o_ref[0, i * tq:i * tq + hq, :] = top[2] / top[1]
            o_ref[0, i * tq + hq:(i + 1) * tq, :] = bot[2] / bot[1]


def _flash(q, k, v, bsz, seq, tq):
    nq = seq // tq
    hb = lambda w: pl.BlockSpec((1, 1, seq, w), lambda b, h: (b, h, 0, 0))
    return pl.pallas_call(
        functools.partial(_flash_body, tq=tq, nq=nq),
        grid=(bsz, C_HEADS),
        in_specs=[hb(C_QK_PAD), hb(C_QK_PAD), hb(2 * C_V_DIM)],
        out_specs=pl.BlockSpec((1, seq, C_V_DIM), lambda b, h: (b, 0, h)),
        out_shape=jax.ShapeDtypeStruct((bsz, seq, C_WIDTH), F32),
        compiler_params=_cparams(("parallel", "parallel")),
        name="mla_flash",
    )(q, k, v)


def _ffn_body(x_ref, ya_ref, o1_ref, o2_ref, o3_ref, l1_ref, l2_ref, l3_ref, yc_ref,
              wo_ref, g_ref, wg_ref, wu_ref, wd_ref, fg_ref, out_ref, h_s, slab_s, *, final):
    tm = x_ref.shape[0]

    def natural(dil, o_ref, l_ref):
        gs = tm // dil
        for r in range(dil):
            rows = pl.ds(r, gs, stride=dil)
            ov = o_ref[r * gs:(r + 1) * gs, :].astype(F32)
            lv = l_ref[r * gs:(r + 1) * gs, :]
            slab_s[0, rows, :] = ov[:, :LANES]
            slab_s[1, rows, :] = ov[:, LANES:]
            slab_s[2, rows, :] = lv[:, :LANES]
            slab_s[3, rows, :] = lv[:, LANES:]
        return (jnp.concatenate([slab_s[0], slab_s[1]], axis=1),
                jnp.concatenate([slab_s[2], slab_s[3]], axis=1))

    mix = (jnp.dot(ya_ref[...].astype(BF16), wo_ref[0:A_WIDTH, :], preferred_element_type=F32)
           + jnp.dot(yc_ref[...].astype(BF16), wo_ref[A_WIDTH + B_WIDTH:, :],
                     preferred_element_type=F32))
    o1, l1 = o1_ref[...].astype(F32), l1_ref[...]
    o2, l2 = natural(B_PATTERNS[1][1], o2_ref, l2_ref)
    o3, l3 = natural(B_PATTERNS[2][1], o3_ref, l3_ref)
    mx = jnp.maximum(jnp.maximum(l1, l2), l3)
    e1, e2, e3 = jnp.exp(l1 - mx), jnp.exp(l2 - mx), jnp.exp(l3 - mx)
    yb = (e1 * o1 + e2 * o2 + e3 * o3) / (e1 + e2 + e3)
    mix = mix + jnp.dot(yb.astype(BF16), wo_ref[A_WIDTH:A_WIDTH + B_WIDTH, :],
                        preferred_element_type=F32)
    xn = x_ref[...] + mix
    out_ref[...] = xn
    h_s[...] = _rms(xn, g_ref[...]).astype(BF16)

    for j in range(D_FF // FF_CHUNK):
        cols = slice(j * FF_CHUNK, (j + 1) * FF_CHUNK)
        h = h_s[...]
        gt = jnp.dot(h, wg_ref[:, cols], preferred_element_type=F32)
        up = jnp.dot(h, wu_ref[:, cols], preferred_element_type=F32)
        act = gt * _sigmoid(gt) * up
        out_ref[...] += jnp.dot(act.astype(BF16), wd_ref[cols, :], preferred_element_type=F32)
    if final:
        out_ref[...] = _rms(out_ref[...], fg_ref[...])


def _ffn(x2, ya, ob, lb, yc, wo, g, wg, wu, wd, fg, tm, final):
    n = x2.shape[0]
    row = lambda i: (i, 0)
    rb = lambda w: pl.BlockSpec((tm, w), row)
    res = lambda a: pl.BlockSpec(a.shape, lambda i: (0,) * a.ndim, pipeline_mode=pl.Buffered(1))
    return pl.pallas_call(
        functools.partial(_ffn_body, final=final),
        grid=(n // tm,),
        in_specs=[rb(D_MODEL), rb(A_WIDTH), rb(B_WIDTH), rb(B_WIDTH), rb(B_WIDTH),
                  rb(B_WIDTH), rb(B_WIDTH), rb(B_WIDTH), rb(C_WIDTH),
                  res(wo), res(g), res(wg), res(wu), res(wd), res(fg)],
        out_specs=pl.BlockSpec((tm, D_MODEL), row),
        out_shape=jax.ShapeDtypeStruct((n, D_MODEL), F32),
        scratch_shapes=[pltpu.VMEM((tm, D_MODEL), BF16), pltpu.VMEM((4, tm, LANES), F32)],
        compiler_params=_cparams(("parallel",)),
        name="out_proj_ffn",
    )(x2, ya, ob[0], ob[1], ob[2], lb[0], lb[1], lb[2], yc, wo, g, wg, wu, wd, fg)


def _pad_cols(w, width):
    return jnp.pad(w, ((0, 0), (0, width - w.shape[1])))


def _rope_layout_cols(w):
    z = jnp.zeros(w.shape[:-1] + (32,), w.dtype)
    return jnp.concatenate([w[..., :32], z, w[..., 32:], z], axis=-1)


def _layer_params(l, w_in, a_mu, a_w0, a_decay_up, a_a0, a_iclr_up, a_gate_up, a_k_k, a_k_a,
                  a_r_k, a_ln_w, a_ln_b, c_w_uq, c_w_ukv):
    wi = w_in[l]
    oa = A_PROJ
    ob = oa + 3 * B_WIDTH
    w_all = jnp.concatenate([
        _pad_cols(wi[:, :oa], A_PAD),
        wi[:, oa:ob],
        wi[:, ob:ob + C_Q_LORA + C_KV_LORA],
        _rope_layout_cols(wi[:, ob + C_Q_LORA + C_KV_LORA:]),
    ], axis=1).astype(BF16)
    z64 = jnp.zeros((64, A_WIDTH), F32)
    rw = lambda t: t.reshape(1, -1)
    wq = c_w_uq[l].reshape(C_Q_LORA, C_HEADS, C_NOPE_DIM + C_ROPE_DIM)
    wq = jnp.concatenate([wq[..., :C_NOPE_DIM], _rope_layout_cols(wq[..., C_NOPE_DIM:])], axis=-1)
    wkv = c_w_ukv[l].reshape(C_KV_LORA, C_HEADS, C_NOPE_DIM + C_V_DIM)
    wkv = jnp.concatenate([wkv[..., :C_NOPE_DIM].reshape(C_KV_LORA, -1),
                           wkv[..., C_NOPE_DIM:].reshape(C_KV_LORA, -1)], axis=1)
    return dict(
        w_all=w_all,
        mu=_pad_cols(rw(a_mu[l]), A_PAD), w0=rw(a_w0[l]),
        dup=jnp.concatenate([a_decay_up[l], z64], axis=0),
        a0=rw(a_a0[l]),
        iup=jnp.concatenate([z64, a_iclr_up[l]], axis=0),
        gup=jnp.pad(a_gate_up[l], ((0, A_WIDTH - A_GATE_LORA), (0, 0))),
        k_k=rw(a_k_k[l]), k_a=rw(a_k_a[l]), r_k=rw(a_r_k[l]),
        ln_w=rw(a_ln_w[l]), ln_b=rw(a_ln_b[l]),
        wq=wq.reshape(C_Q_LORA, C_HEADS * C_QK_PAD).astype(BF16),
        wkv=wkv.astype(BF16),
    )


def _rope_body(pos_ref, freq_ref, cb_ref, sb_ref, cc_ref, sc_ref):
    half = LANES // 2
    hm = pos_ref.shape[0]
    lane = lax.broadcasted_iota(jnp.int32, (1, LANES), 1)
    pos = pos_ref[...]
    ang = jnp.where(lane < half, pos[:, 0:1], pos[:, 1:2]) * freq_ref[...]
    c2 = jnp.cos(ang)
    s2 = jnp.sin(ang)
    nc, nb = C_ROPE_DIM // 2, B_ROT_DIM // 2
    is_c = lane < nc
    is_b = (lane >= nc) & (lane < nc + nb)
    rot = (lane % B_HEAD_DIM) < B_ROT_DIM
    for par in range(2):
        rows = pl.ds(par, hm, stride=2)
        c = c2 if par == 0 else pltpu.roll(c2, half, 1)
        s = s2 if par == 0 else pltpu.roll(s2, half, 1)
        c0 = jnp.where(is_c, c, 0.0)
        s0 = jnp.where(is_c, s, 0.0)
        cc_ref[rows, :] = c0 + pltpu.roll(c0, half, 1)
        sc_ref[rows, :] = pltpu.roll(s0, half, 1) - s0
        cb0 = pltpu.roll(jnp.where(is_b, c, 0.0), LANES - nc, 1)
        sb0 = pltpu.roll(jnp.where(is_b, s, 0.0), LANES - nc, 1)
        cb1 = cb0 + pltpu.roll(cb0, nb, 1)
        sb1 = pltpu.roll(sb0, nb, 1) - sb0
        cb_ref[rows, :] = jnp.where(rot, cb1 + pltpu.roll(cb1, B_HEAD_DIM, 1), 1.0)
        sb_ref[rows, :] = sb1 + pltpu.roll(sb1, B_HEAD_DIM, 1)


def _rope_tables(positions, tm):
    n = positions.size
    pos = positions.reshape(n // 2, 2).astype(F32)
    inv_freq = lambda dim: 1.0 / (ROPE_THETA ** (jnp.arange(0, dim, 2, dtype=F32) / dim))
    half = jnp.concatenate([inv_freq(C_ROPE_DIM), inv_freq(B_ROT_DIM),
                            jnp.zeros((LANES // 2 - C_ROPE_DIM // 2 - B_ROT_DIM // 2,), F32)])
    freq = jnp.concatenate([half, half]).reshape(1, LANES)
    tab = pl.BlockSpec((tm, LANES), lambda i: (i, 0))
    return pl.pallas_call(
        _rope_body,
        grid=(n // tm,),
        in_specs=[pl.BlockSpec((tm // 2, 2), lambda i: (i, 0)), pl.BlockSpec((1, LANES), lambda i: (0, 0))],
        out_specs=[tab] * 4,
        out_shape=[jax.ShapeDtypeStruct((n, LANES), F32)] * 4,
        compiler_params=_cparams(("parallel",)),
        name="rope_tables",
    )(pos, freq)


def kernel(x, positions, attn_norm_g, w_in, a_mu, a_w0, a_decay_up, a_a0, a_iclr_up, a_gate_up, a_k_k, a_k_a, a_r_k, a_ln_w, a_ln_b, c_q_norm_g, c_kv_norm_g, c_w_uq, c_w_ukv, w_out, ffn_norm_g, ffn_w_gate, ffn_w_up, ffn_w_down, final_norm_g):
    bsz, seq, _ = x.shape
    n = bsz * seq
    depth = w_in.shape[0]
    assert seq % max(w for w, _ in B_PATTERNS) == 0
    x2 = x.reshape(n, D_MODEL)
    tm = min(ROW_TILE, seq)
    cb, sb, cc, sc = _rope_tables(positions, 4 * tm)
    for l in range(depth):
        p = _layer_params(l, w_in, a_mu, a_w0, a_decay_up, a_a0, a_iclr_up, a_gate_up, a_k_k,
                          a_k_a, a_r_k, a_ln_w, a_ln_b, c_w_uq, c_w_ukv)
        pa, *qkv = _in_proj(x2, attn_norm_g[l].reshape(1, -1), p["w_all"], cb, sb,
                            c_q_norm_g[l].reshape(1, -1), c_kv_norm_g[l].reshape(1, -1),
                            p["wq"], p["wkv"], cc, sc, bsz, seq, tm)
        qc, kc, vc = qkv[9:]
        ya = _rwkv(pa, p, bsz, seq, min(RWKV_TILE, seq))
        ob, lb = zip(*[_dilated_branch(*qkv[3 * i:3 * i + 3], bsz, seq, d, tm)
                       for i, (_, d) in enumerate(B_PATTERNS)])
        yc = _flash(qc, kc, vc, bsz, seq, min(FLASH_TILE, seq)).reshape(n, C_WIDTH)
        x2 = _ffn(x2, ya, ob, lb, yc, w_out[l].astype(BF16), ffn_norm_g[l].reshape(1, -1),
                  ffn_w_gate[l].astype(BF16), ffn_w_up[l].astype(BF16), ffn_w_down[l].astype(BF16),
                  final_norm_g.reshape(1, -1), tm, l == depth - 1)
    return x2.reshape(bsz, seq, D_MODEL)
```
